```python
import math, functools
import jax, jax.numpy as jnp
from jax import lax
import numpy as np

D_MODEL = 1024
BATCH = 16
SEQ = 2048
DEPTH = 1
DEC_BATCH = 128
DEC_SEQ = 1
PAST_LEN = 8192
PAGE_SIZE = 128

GLA_HEADS = 4
GLA_DK = 64
GLA_DV = 128
GLA_GATE_RANK = 16
GLA_GATE_NORM = 16.0
GLA_CHUNK = 64
MLA_HEADS = 4
MLA_NOPE = 128
MLA_ROPE = 64
MLA_V = 128
MLA_Q_RANK = 256
MLA_KV_RANK = 128
ROPE_THETA = 10000.0
Q_BLOCK = 128
MLA_SCALE = 1.0 / math.sqrt(MLA_NOPE + MLA_ROPE)
D_FF = 2816
EPS = 1e-6

GLA_WIDTH = GLA_HEADS * GLA_DV
MLA_WIDTH = MLA_HEADS * MLA_V
MIX_WIDTH = GLA_WIDTH + MLA_WIDTH
IN_SPLITS = (GLA_HEADS * GLA_DK, GLA_HEADS * GLA_DK, GLA_WIDTH, GLA_WIDTH, GLA_GATE_RANK,
             MLA_Q_RANK, MLA_KV_RANK, MLA_ROPE)
IN_WIDTH = sum(IN_SPLITS)

kernel_name = "hymba_gla_mla_macaron_step"


def rms_norm(x, w):
    xf = x.astype(jnp.float32)
    y = xf * lax.rsqrt(jnp.mean(xf * xf, axis=-1, keepdims=True) + EPS)
    return (y * w.astype(jnp.float32)).astype(x.dtype)


def swiglu(x, w_gate, w_up, w_down):
    return (jax.nn.silu(x @ w_gate) * (x @ w_up)) @ w_down


def rope(x, pos):
    half = MLA_ROPE // 2
    inv_freq = jnp.power(jnp.float32(ROPE_THETA), -jnp.arange(half, dtype=jnp.float32) / half)
    ang = pos.astype(jnp.float32)[:, None] * inv_freq[None, :]
    cos = jnp.cos(ang)[None, :, None, :]
    sin = jnp.sin(ang)[None, :, None, :]
    xf = x.astype(jnp.float32)
    x1, x2 = xf[..., :half], xf[..., half:]
    return jnp.concatenate([x1 * cos - x2 * sin, x1 * sin + x2 * cos], axis=-1).astype(x.dtype)


def gla_inputs(q, k, v, a_low, w_a_up, b_a):
    B, T = q.shape[:2]
    def to_heads(t, d):
        return t.reshape(B, T, GLA_HEADS, d).transpose(0, 2, 1, 3)
    log_a = jax.nn.log_sigmoid((a_low @ w_a_up + b_a).astype(jnp.float32)) / GLA_GATE_NORM
    return (to_heads(q, GLA_DK) * (GLA_DK ** -0.5), to_heads(k, GLA_DK),
            to_heads(v, GLA_DV), to_heads(log_a, GLA_DK))


def gla_chunk(S, q, k, v, log_a):
    C = q.shape[2]
    b = jnp.cumsum(log_a, axis=2)
    causal = jnp.tril(jnp.ones((C, C), dtype=bool))
    diff = b[:, :, :, None, :] - b[:, :, None, :, :]
    decay = jnp.exp(jnp.where(causal[:, :, None], diff, -jnp.inf))
    attn = jnp.einsum('bhid,bhjd,bhijd->bhij', q, k, decay)
    o = (jnp.einsum('bhij,bhjv->bhiv', attn, v)
         + jnp.einsum('bhid,bhdv->bhiv', q * jnp.exp(b), S))
    b_last = b[:, :, -1:, :]
    S_new = (jnp.exp(b_last[:, :, 0, :])[..., None] * S
             + jnp.einsum('bhjd,bhjv->bhdv', k * jnp.exp(b_last - b), v))
    return S_new.astype(jnp.float32), o.astype(jnp.float32)


def gla_prompt(q, k, v, log_a):
    B, H, T, _ = q.shape
    n = T // GLA_CHUNK
    def chunks(t):
        return t.reshape(B, H, n, GLA_CHUNK, t.shape[-1]).transpose(2, 0, 1, 3, 4)
    S0 = jnp.zeros((B, H, GLA_DK, GLA_DV), jnp.float32)
    S, o = lax.scan(lambda s, c: gla_chunk(s, *c), S0, (chunks(q), chunks(k), chunks(v), chunks(log_a)))
    o = o.transpose(1, 2, 0, 3, 4).reshape(B, H, T, GLA_DV)
    return o, S


def gla_sample(S0, q, k, v, log_a):
    S, o = gla_chunk(S0.astype(jnp.float32), q, k, v, log_a)
    return o, S


def gla_output(o, g, norm_w):
    B, H, T, _ = o.shape
    o = rms_norm(o.transpose(0, 2, 1, 3), norm_w)
    return (o * jax.nn.silu(g.astype(jnp.float32)).reshape(B, T, H, GLA_DV)).reshape(B, T, GLA_WIDTH).astype(g.dtype)


def mla_project(c_q, c_kv, k_pe_raw, pos, q_norm_w, kv_norm_w, w_uq):
    B, T = c_q.shape[:2]
    q = (rms_norm(c_q, q_norm_w) @ w_uq).reshape(B, T, MLA_HEADS, MLA_NOPE + MLA_ROPE)
    q_nope = q[..., :MLA_NOPE]
    q_pe = rope(q[..., MLA_NOPE:], pos)
    lat = rms_norm(c_kv, kv_norm_w)
    k_pe = rope(k_pe_raw[:, :, None, :], pos)[:, :, 0, :]
    return q_nope, q_pe, lat, k_pe


def mla_prompt_attention(q_nope, q_pe, lat, k_pe, w_uk, w_uv):
    B, T = q_nope.shape[:2]
    k_nope = jnp.einsum('btc,chd->bthd', lat, w_uk)
    v = jnp.einsum('btc,chd->bthd', lat, w_uv)
    nb = T // Q_BLOCK
    qn_b = q_nope.reshape(B, nb, Q_BLOCK, MLA_HEADS, MLA_NOPE).transpose(1, 0, 2, 3, 4)
    qp_b = q_pe.reshape(B, nb, Q_BLOCK, MLA_HEADS, MLA_ROPE).transpose(1, 0, 2, 3, 4)
    starts = jnp.arange(nb, dtype=jnp.int32) * Q_BLOCK
    kpos = jnp.arange(T, dtype=jnp.int32)

    def block(args):
        qn, qp, start = args
        s = (jnp.einsum('bqhd,bkhd->bhqk', qn, k_nope)
             + jnp.einsum('bqhr,bkr->bhqk', qp, k_pe)).astype(jnp.float32) * MLA_SCALE
        qpos = start + jnp.arange(Q_BLOCK, dtype=jnp.int32)
        s = jnp.where(kpos[None, :] <= qpos[:, None], s, -jnp.inf)
        p = jax.nn.softmax(s, axis=-1)
        return jnp.einsum('bhqk,bkhd->bqhd', p.astype(v.dtype), v)

    o = lax.map(block, (qn_b, qp_b, starts))
    return o.transpose(1, 0, 2, 3, 4).reshape(B, T, MLA_WIDTH)


def mla_sample_attention(past_lat, past_pe, q_nope, q_pe, lat, k_pe, w_uk, w_uv):
    B, T = q_nope.shape[:2]
    past_len = past_lat.shape[1]
    lat_all = jnp.concatenate([past_lat.astype(lat.dtype), lat], axis=1)
    pe_all = jnp.concatenate([past_pe.astype(k_pe.dtype), k_pe], axis=1)
    q_lat = jnp.einsum('bqhd,chd->bqhc', q_nope, w_uk)
    s = (jnp.einsum('bqhc,bkc->bhqk', q_lat, lat_all)
         + jnp.einsum('bqhr,bkr->bhqk', q_pe, pe_all)).astype(jnp.float32) * MLA_SCALE
    kpos = jnp.arange(lat_all.shape[1], dtype=jnp.int32)
    qpos = past_len + jnp.arange(T, dtype=jnp.int32)
    s = jnp.where(kpos[None, :] <= qpos[:, None], s, -jnp.inf)
    p = jax.nn.softmax(s, axis=-1)
    o_lat = jnp.einsum('bhqk,bkc->bqhc', p.astype(lat_all.dtype), lat_all)
    return jnp.einsum('bqhc,chd->bqhd', o_lat, w_uv).reshape(B, T, MLA_WIDTH)


def hybrid_layer(x, pos, lw, gla_mix, mla_attend):
    (f1n, f1g, f1u, f1d, mix_n, w_in, w_a_up, b_a, g_norm, q_norm, w_uq, kv_norm,
     w_uk, w_uv, w_out, f2n, f2g, f2u, f2d) = lw
    x = x + 0.5 * swiglu(rms_norm(x, f1n), f1g, f1u, f1d)
    h = rms_norm(x, mix_n)
    offsets = [int(o) for o in np.cumsum(IN_SPLITS)[:-1]]
    q, k, v, g, a_low, c_q, c_kv, k_pe_raw = jnp.split(h @ w_in, offsets, axis=-1)
    o_gla, S = gla_mix(*gla_inputs(q, k, v, a_low, w_a_up, b_a))
    y_gla = gla_output(o_gla, g, g_norm)
    q_nope, q_pe, lat, k_pe = mla_project(c_q, c_kv, k_pe_raw, pos, q_norm, kv_norm, w_uq)
    y_mla = mla_attend(q_nope, q_pe, lat, k_pe, w_uk, w_uv)
    x = x + jnp.concatenate([y_gla, y_mla], axis=-1) @ w_out
    x = x + 0.5 * swiglu(rms_norm(x, f2n), f2g, f2u, f2d)
    return x, lat, k_pe, S


def setup_inputs(seed: int = 0) -> dict:
    key = jax.random.key(seed)
    ks = iter(jax.random.split(key, 40))
    N_PAGES = PAST_LEN // PAGE_SIZE
    N_POOL = (DEC_BATCH * N_PAGES * 5) // 4
    f32 = jnp.float32

    def nrm(shape, scale):
        return jax.random.normal(next(ks), shape, f32) * scale

    def gain(shape):
        return 1.0 + 0.01 * jax.random.normal(next(ks), shape, f32)

    x_prompt = jax.random.normal(next(ks), (BATCH, SEQ, D_MODEL), f32)
    x_sample = jax.random.normal(next(ks), (DEC_BATCH, DEC_SEQ, D_MODEL), f32)
    cache_kv = jax.random.normal(next(ks), (DEPTH, N_POOL, PAGE_SIZE, MLA_KV_RANK), f32)
    cache_pe = jax.random.normal(next(ks), (DEPTH, N_POOL, PAGE_SIZE, MLA_ROPE), f32)
    state_gla = nrm((DEPTH, DEC_BATCH, GLA_HEADS, GLA_DK, GLA_DV), 0.5)
    page_table = jax.random.permutation(next(ks), N_POOL)[:DEC_BATCH * N_PAGES].reshape(
        DEC_BATCH, N_PAGES).astype(jnp.int32)
    return {
        "x_prompt": x_prompt,
        "x_sample": x_sample,
        "cache_kv": cache_kv,
        "cache_pe": cache_pe,
        "state_gla": state_gla,
        "page_table": page_table,
        "ffn1_norm_w": gain((DEPTH, D_MODEL)),
        "ffn1_w_gate": nrm((DEPTH, D_MODEL, D_FF), D_MODEL ** -0.5),
        "ffn1_w_up": nrm((DEPTH, D_MODEL, D_FF), D_MODEL ** -0.5),
        "ffn1_w_down": nrm((DEPTH, D_FF, D_MODEL), D_FF ** -0.5),
        "mix_norm_w": gain((DEPTH, D_MODEL)),
        "w_in": nrm((DEPTH, D_MODEL, IN_WIDTH), D_MODEL ** -0.5),
        "gla_w_a_up": nrm((DEPTH, GLA_GATE_RANK, GLA_HEADS * GLA_DK), GLA_GATE_RANK ** -0.5),
        "gla_b_a": nrm((DEPTH, GLA_HEADS * GLA_DK), 0.1),
        "gla_norm_w": gain((DEPTH, GLA_DV)),
        "mla_q_norm_w": gain((DEPTH, MLA_Q_RANK)),
        "mla_w_uq": nrm((DEPTH, MLA_Q_RANK, MLA_HEADS * (MLA_NOPE + MLA_ROPE)), MLA_Q_RANK ** -0.5),
        "mla_kv_norm_w": gain((DEPTH, MLA_KV_RANK)),
        "mla_w_uk": nrm((DEPTH, MLA_KV_RANK, MLA_HEADS, MLA_NOPE), MLA_KV_RANK ** -0.5),
        "mla_w_uv": nrm((DEPTH, MLA_KV_RANK, MLA_HEADS, MLA_V), MLA_KV_RANK ** -0.5),
        "w_out": nrm((DEPTH, MIX_WIDTH, D_MODEL), MIX_WIDTH ** -0.5),
        "ffn2_norm_w": gain((DEPTH, D_MODEL)),
        "ffn2_w_gate": nrm((DEPTH, D_MODEL, D_FF), D_MODEL ** -0.5),
        "ffn2_w_up": nrm((DEPTH, D_MODEL, D_FF), D_MODEL ** -0.5),
        "ffn2_w_down": nrm((DEPTH, D_FF, D_MODEL), D_FF ** -0.5),
        "final_norm_w": gain((D_MODEL,)),
    }


def reference(x_prompt, x_sample, cache_kv, cache_pe, state_gla, page_table,
              ffn1_norm_w, ffn1_w_gate, ffn1_w_up, ffn1_w_down, mix_norm_w, w_in,
              gla_w_a_up, gla_b_a, gla_norm_w, mla_q_norm_w, mla_w_uq, mla_kv_norm_w,
              mla_w_uk, mla_w_uv, w_out, ffn2_norm_w, ffn2_w_gate, ffn2_w_up, ffn2_w_down,
              final_norm_w):
    B_s, T_s = x_sample.shape[:2]
    pos_prompt = jnp.arange(x_prompt.shape[1], dtype=jnp.int32)
    pos_sample = PAST_LEN + jnp.arange(T_s, dtype=jnp.int32)
    layer_weights = (ffn1_norm_w, ffn1_w_gate, ffn1_w_up, ffn1_w_down, mix_norm_w, w_in,
                     gla_w_a_up, gla_b_a, gla_norm_w, mla_q_norm_w, mla_w_uq, mla_kv_norm_w,
                     mla_w_uk, mla_w_uv, w_out, ffn2_norm_w, ffn2_w_gate, ffn2_w_up, ffn2_w_down)
    xp, xs = x_prompt, x_sample
    lat_p, pe_p, gla_p, lat_s, pe_s, gla_s = [], [], [], [], [], []
    for l in range(DEPTH):
        lw = tuple(w[l] for w in layer_weights)
        xp, lat, kpe, S = hybrid_layer(xp, pos_prompt, lw, gla_prompt, mla_prompt_attention)
        lat_p.append(lat); pe_p.append(kpe); gla_p.append(S)
        past_lat = cache_kv[l, page_table].reshape(B_s, -1, MLA_KV_RANK)
        past_pe = cache_pe[l, page_table].reshape(B_s, -1, MLA_ROPE)
        xs, lat, kpe, S = hybrid_layer(
            xs, pos_sample, lw,
            functools.partial(gla_sample, state_gla[l]),
            functools.partial(mla_sample_attention, past_lat, past_pe))
        lat_s.append(lat); pe_s.append(kpe); gla_s.append(S)
    y_prompt = rms_norm(xp, final_norm_w)
    y_sample = rms_norm(xs, final_norm_w)
    new_kv_prompt = jnp.stack(lat_p)
    new_pe_prompt = jnp.stack(pe_p)
    new_gla_prompt = jnp.stack(gla_p)
    new_kv_sample = jnp.stack(lat_s)
    new_pe_sample = jnp.stack(pe_s)
    new_gla_sample = jnp.stack(gla_s)
    return (y_prompt, y_sample, new_kv_prompt, new_pe_prompt, new_gla_prompt,
            new_kv_sample, new_pe_sample, new_gla_sample)
```

```python
import functools
import math

import jax
import jax.numpy as jnp
from jax import lax
from jax.experimental import pallas as pl
from jax.experimental.pallas import tpu as pltpu

F32 = jnp.float32
BF16 = jnp.bfloat16

EPS = 1e-6
GLA_HEADS = 4
GLA_DK = 64
GLA_DV = 128
GLA_GATE_RANK = 16
GLA_GATE_NORM = 16.0
MLA_HEADS = 4
MLA_NOPE = 128
MLA_ROPE = 64
MLA_V = 128
MLA_Q_RANK = 256
MLA_KV_RANK = 128
ROPE_THETA = 10000.0
MLA_SCALE = 1.0 / math.sqrt(MLA_NOPE + MLA_ROPE)

LANES = 128
GLA_CHUNK = 128
GLA_PAIR = 2 * GLA_DK
VMEM_LIMIT = 56 * 1024 * 1024


def _rms(x, w):
    return x * lax.rsqrt(jnp.mean(x * x, axis=-1, keepdims=True) + EPS) * w


def _silu(x):
    return x * jax.nn.sigmoid(x)


def _const_spec(shape):
    nd = len(shape)
    return pl.BlockSpec(shape, lambda *_: (0,) * nd, pipeline_mode=pl.Buffered(1))


def _params(*sem):
    return pltpu.CompilerParams(dimension_semantics=sem, vmem_limit_bytes=VMEM_LIMIT)


def _ffn_kernel(*refs, has_mix, has_final):
    refs = list(refs)
    x_ref = refs.pop(0)
    if has_mix:
        ya_ref, yb_ref, wo_ref = refs.pop(0), refs.pop(0), refs.pop(0)
    nw_ref, wg_ref, wu_ref, wd_ref = refs.pop(0), refs.pop(0), refs.pop(0), refs.pop(0)
    if has_final:
        fw_ref = refs.pop(0)
    (o_ref,) = refs

    x = x_ref[...]
    if has_mix:
        y = jnp.concatenate([ya_ref[...], yb_ref[...]], axis=-1)
        x = x + jnp.dot(y, wo_ref[...], preferred_element_type=F32)
    h = _rms(x, nw_ref[...]).astype(BF16)
    g = jnp.dot(h, wg_ref[...], preferred_element_type=F32)
    u = jnp.dot(h, wu_ref[...], preferred_element_type=F32)
    a = (_silu(g) * u).astype(BF16)
    x = x + 0.5 * jnp.dot(a, wd_ref[...], preferred_element_type=F32)
    if has_final:
        x = _rms(x, fw_ref[...])
    o_ref[...] = x


def _ffn(x, norm_w, wg, wu, wd, *, mix=None, final_w=None, tm):
    m, d = x.shape
    ff = wg.shape[1]
    tm = min(tm, m)
    row = lambda i: (i, 0)
    args, specs = [x], [pl.BlockSpec((tm, d), row)]
    if mix is not None:
        ya, yb, wo = mix
        args += [ya, yb, wo]
        specs += [pl.BlockSpec((tm, ya.shape[1]), row), pl.BlockSpec((tm, yb.shape[1]), row),
                  _const_spec(wo.shape)]
    args += [norm_w.reshape(1, d), wg, wu, wd]
    specs += [_const_spec((1, d)), _const_spec((d, ff)), _const_spec((d, ff)), _const_spec((ff, d))]
    if final_w is not None:
        args.append(final_w.reshape(1, d))
        specs.append(_const_spec((1, d)))
    return pl.pallas_call(
        functools.partial(_ffn_kernel, has_mix=mix is not None, has_final=final_w is not None),
        grid=(m // tm,),
        in_specs=specs,
        out_specs=pl.BlockSpec((tm, d), row),
        out_shape=jax.ShapeDtypeStruct((m, d), F32),
        compiler_params=_params("parallel"),
        name="ffn_mix_final" if mix is not None else "ffn",
    )(*args)


_IN_GQ, _IN_GK, _IN_GV, _IN_GG, _IN_CQ, _IN_CKV, _IN_KPE, _IN_KROT, _IN_ALOW, _IN_END = (
    0, 256, 512, 1024, 1536, 1792, 1920, 2048, 2176, 2304)
_UQ_HEAD = 3 * LANES
_QK_HEAD = 2 * LANES
_Q_ROWS = 16


def _inproj_kernel(x_ref, nw_ref, win_ref, wa_ref, ba_ref, qn_ref, wuq_ref, kvn_ref, wukv_ref,
                   cos_ref, sin_ref,
                   gq_ref, gk_ref, gla_ref, gv_ref, gg_ref, mq_ref, mk_ref, mv_ref, lat_ref, kpe_ref):
    h = _rms(x_ref[...], nw_ref[...]).astype(BF16)
    p = jnp.dot(h, win_ref[...], preferred_element_type=F32)
    cos, sin = cos_ref[...], sin_ref[...]

    gq_ref[...] = p[:, _IN_GQ:_IN_GK] * (GLA_DK ** -0.5)
    gk_ref[...] = p[:, _IN_GK:_IN_GV]
    gv_ref[...] = p[:, _IN_GV:_IN_GG].astype(BF16)
    gg_ref[...] = p[:, _IN_GG:_IN_CQ]
    z = jnp.dot(p[:, _IN_ALOW:_IN_END].astype(BF16), wa_ref[...], preferred_element_type=F32) + ba_ref[...]
    log_sig = jnp.minimum(z, 0.0) - jnp.log1p(jnp.exp(-jnp.abs(z)))
    gla_ref[...] = log_sig * (1.0 / GLA_GATE_NORM)

    cq = _rms(p[:, _IN_CQ:_IN_CKV], qn_ref[...]).astype(BF16)
    qx = jnp.dot(cq, wuq_ref[...], preferred_element_type=F32)
    lat = _rms(p[:, _IN_CKV:_IN_KPE], kvn_ref[...])
    lat_ref[...] = lat
    kpe = p[:, _IN_KPE:_IN_KROT] * cos + p[:, _IN_KROT:_IN_ALOW] * sin
    kpe_ref[...] = kpe[:, :MLA_ROPE]
    kv = jnp.dot(lat.astype(BF16), wukv_ref[...], preferred_element_type=F32)
    kpe_b = kpe.astype(BF16)
    for hd in range(MLA_HEADS):
        q0 = hd * _UQ_HEAD
        q_pe = qx[:, q0 + LANES:q0 + 2 * LANES] * cos + qx[:, q0 + 2 * LANES:q0 + 3 * LANES] * sin
        o0 = hd * _QK_HEAD
        mq_ref[:, o0:o0 + LANES] = qx[:, q0:q0 + LANES].astype(BF16)
        mq_ref[:, o0 + LANES:o0 + 2 * LANES] = q_pe.astype(BF16)
        mk_ref[:, o0:o0 + LANES] = kv[:, hd * MLA_NOPE:(hd + 1) * MLA_NOPE].astype(BF16)
        mk_ref[:, o0 + LANES:o0 + 2 * LANES] = kpe_b
    mv_ref[...] = kv[:, MLA_HEADS * MLA_NOPE:].astype(BF16)


def _inproj(x, norm_w, w, cos, sin, *, tm):
    m, d = x.shape
    tm = min(tm, m)
    n_tab = cos.shape[0] // tm
    row = lambda i: (i, 0)
    tab = lambda i: (i % n_tab, 0)
    gw, mw = GLA_HEADS * GLA_DK, MLA_HEADS * _QK_HEAD
    outs = [((m, gw), F32), ((m, gw), F32), ((m, gw), F32), ((m, GLA_HEADS * GLA_DV), BF16),
            ((m, GLA_HEADS * GLA_DV), F32), ((m, mw), BF16), ((m, mw), BF16),
            ((m, MLA_HEADS * MLA_V), BF16), ((m, MLA_KV_RANK), F32), ((m, MLA_ROPE), F32)]
    consts = [norm_w.reshape(1, d), w["w_in"], w["w_a"], w["b_a"], w["q_norm"], w["w_uq"],
              w["kv_norm"], w["w_ukv"]]
    return pl.pallas_call(
        _inproj_kernel,
        grid=(m // tm,),
        in_specs=[pl.BlockSpec((tm, d), row)] + [_const_spec(c.shape) for c in consts]
        + [pl.BlockSpec((tm, LANES), tab), pl.BlockSpec((tm, LANES), tab)],
        out_specs=[pl.BlockSpec((tm, s[1]), row) for s, _ in outs],
        out_shape=[jax.ShapeDtypeStruct(s, dt) for s, dt in outs],
        compiler_params=_params("parallel"),
        name="inproj",
    )(x, *consts, cos, sin)


def _split3(x):
    hi = x.astype(BF16)
    r = x - hi.astype(F32)
    mid = r.astype(BF16)
    lo = (r - mid.astype(F32)).astype(BF16)
    return hi, mid, lo


def _gla_prompt_kernel(q_ref, k_ref, la_ref, v_ref, g_ref, gn_ref, y_ref, s_ref):
    c_sz = GLA_CHUNK

    @pl.when(pl.program_id(1) == 0)
    def _():
        s_ref[...] = jnp.zeros_like(s_ref)

    row = lax.broadcasted_iota(jnp.int32, (c_sz, c_sz), 0)
    col = lax.broadcasted_iota(jnp.int32, (c_sz, c_sz), 1)
    tri = row >= col
    tri_b = jnp.where(tri, 1.0, 0.0).astype(BF16)
    lane_lo = col < GLA_DK
    row_lo = row < GLA_DK
    gn = gn_ref[...]

    def chunk(c, carry):
        r0 = pl.multiple_of(c * c_sz, c_sz)
        rows = pl.ds(r0, c_sz)
        la = la_ref[0, rows, :]
        b = sum(jnp.dot(tri_b, t, preferred_element_type=F32) for t in _split3(la))
        e_pos = jnp.exp(b)
        qp = q_ref[0, rows, :] * e_pos
        kp = k_ref[0, rows, :] * jnp.exp(-b)
        kd = kp * e_pos[c_sz - 1:c_sz, :]
        e_last_col = jnp.exp(jnp.sum(la.T, axis=1, keepdims=True))
        v = v_ref[0, rows, :]
        g = g_ref[0, rows, :]
        for pr in range(GLA_HEADS // 2):
            sl = slice(pr * GLA_PAIR, (pr + 1) * GLA_PAIR)
            qp_p = qp[:, sl]
            q_heads = (jnp.where(lane_lo, qp_p, 0.0).astype(BF16), jnp.where(lane_lo, 0.0, qp_p).astype(BF16))
            attn = lax.dot_general(jnp.concatenate(q_heads, axis=0), kp[:, sl].astype(BF16),
                                   (((1,), (1,)), ((), ())), preferred_element_type=F32)
            s_old = s_ref[0, pr]
            s_old_b = s_old.astype(BF16)
            for j in range(2):
                hd = 2 * pr + j
                hs = slice(hd * GLA_DV, (hd + 1) * GLA_DV)
                a_j = jnp.where(tri, attn[j * c_sz:(j + 1) * c_sz], 0.0).astype(BF16)
                o = jnp.dot(jnp.concatenate([a_j, q_heads[j]], axis=1),
                            jnp.concatenate([v[:, hs], s_old_b], axis=0), preferred_element_type=F32)
                y_ref[0, rows, hs] = (_rms(o, gn) * _silu(g[:, hs])).astype(BF16)
            kd_t = kd[:, sl].T
            kd_heads = jnp.concatenate([jnp.where(row_lo, kd_t, 0.0).astype(BF16),
                                        jnp.where(row_lo, 0.0, kd_t).astype(BF16)], axis=1)
            v_pair = jnp.concatenate([v[:, (2 * pr) * GLA_DV:(2 * pr + 1) * GLA_DV],
                                      v[:, (2 * pr + 1) * GLA_DV:(2 * pr + 2) * GLA_DV]], axis=0)
            s_ref[0, pr] = e_last_col[sl] * s_old + jnp.dot(kd_heads, v_pair, preferred_element_type=F32)
        return carry

    lax.fori_loop(0, q_ref.shape[1] // c_sz, chunk, 0)


def _gla_prompt(gq, gk, gla, gv, gg, g_norm, *, batch, tc):
    m = gq.shape[0]
    t = m // batch
    tc = min(tc, t)
    r3 = lambda a: a.reshape(batch, t, a.shape[1])
    blk = lambda width: pl.BlockSpec((1, tc, width), lambda b, i: (b, i, 0))
    kw, vw = GLA_HEADS * GLA_DK, GLA_HEADS * GLA_DV
    y, s = pl.pallas_call(
        _gla_prompt_kernel,
        grid=(batch, t // tc),
        in_specs=[blk(kw), blk(kw), blk(kw), blk(vw), blk(vw), _const_spec((1, GLA_DV))],
        out_specs=[blk(vw), pl.BlockSpec((1, GLA_HEADS // 2, GLA_PAIR, GLA_DV), lambda b, i: (b, 0, 0, 0))],
        out_shape=[jax.ShapeDtypeStruct((batch, t, vw), BF16),
                   jax.ShapeDtypeStruct((batch, GLA_HEADS // 2, GLA_PAIR, GLA_DV), F32)],
        compiler_params=_params("parallel", "arbitrary"),
        name="gla_prompt",
    )(r3(gq), r3(gk), r3(gla), r3(gv), r3(gg), g_norm.reshape(1, GLA_DV))
    return y.reshape(m, vw), s.reshape(batch, GLA_HEADS, GLA_DK, GLA_DV)


def _gla_sample_kernel(q_ref, k_ref, la_ref, v_ref, g_ref, gn_ref, s_ref, y_ref, so_ref):
    nb = q_ref.shape[0]
    q, k = q_ref[...], k_ref[...]
    a = jnp.exp(la_ref[...])
    qa = q * a
    v = v_ref[...].astype(F32)
    g = g_ref[...]
    lane_lo = lax.broadcasted_iota(jnp.int32, (nb, GLA_PAIR), 1) < GLA_DK
    qk = q * k
    qk_heads = (jnp.sum(jnp.where(lane_lo, qk, 0.0), axis=1, keepdims=True),
                jnp.sum(jnp.where(lane_lo, 0.0, qk), axis=1, keepdims=True))
    o = [qk_heads[j] * v[:, j * GLA_DV:(j + 1) * GLA_DV] for j in range(2)]
    for c in range(GLA_PAIR):
        j = c // GLA_DK
        s_c = s_ref[:, c, :]
        o[j] = o[j] + qa[:, c:c + 1] * s_c
        so_ref[:, c, :] = a[:, c:c + 1] * s_c + k[:, c:c + 1] * v[:, j * GLA_DV:(j + 1) * GLA_DV]
    gn = gn_ref[...]
    for j in range(2):
        hs = slice(j * GLA_DV, (j + 1) * GLA_DV)
        y_ref[:, hs] = (_rms(o[j], gn) * _silu(g[:, hs])).astype(BF16)


def _gla_sample(gq, gk, gla, gv, gg, g_norm, state):
    nb = gq.shape[0]
    kw, vw = GLA_HEADS * GLA_DK, GLA_HEADS * GLA_DV
    st = state.reshape(nb, kw, GLA_DV)
    lane_blk = lambda width: pl.BlockSpec((nb, width), lambda p: (0, p))
    st_blk = pl.BlockSpec((nb, GLA_PAIR, GLA_DV), lambda p: (0, p, 0))
    y, s = pl.pallas_call(
        _gla_sample_kernel,
        grid=(GLA_HEADS // 2,),
        in_specs=[lane_blk(GLA_PAIR), lane_blk(GLA_PAIR), lane_blk(GLA_PAIR), lane_blk(2 * GLA_DV),
                  lane_blk(2 * GLA_DV), _const_spec((1, GLA_DV)), st_blk],
        out_specs=[lane_blk(2 * GLA_DV), st_blk],
        out_shape=[jax.ShapeDtypeStruct((nb, vw), BF16), jax.ShapeDtypeStruct((nb, kw, GLA_DV), F32)],
        compiler_params=_params("parallel"),
        name="gla_sample",
    )(gq, gk, gla, gv, gg, g_norm.reshape(1, GLA_DV), st)
    return y, s.reshape(nb, GLA_HEADS, GLA_DK, GLA_DV)


def _mla_prompt_kernel(q_ref, k_ref, v_ref, o_ref, *, tq):
    t = q_ref.shape[1]
    for qi in range(t // tq):
        q0, kend = qi * tq, (qi + 1) * tq
        s = lax.dot_general(q_ref[0, q0:kend, :], k_ref[0, 0:kend, :], (((1,), (1,)), ((), ())),
                            preferred_element_type=F32) * MLA_SCALE
        row = lax.broadcasted_iota(jnp.int32, (tq, kend), 0) + q0
        col = lax.broadcasted_iota(jnp.int32, (tq, kend), 1)
        s = jnp.where(col <= row, s, -jnp.inf)
        p = jnp.exp(s - jnp.max(s, axis=1, keepdims=True))
        denom = jnp.sum(p, axis=1, keepdims=True)
        o = jnp.dot(p.astype(BF16), v_ref[0, 0:kend, :], preferred_element_type=F32)
        o_ref[0, q0:kend, :] = (o / denom).astype(BF16)


def _mla_prompt(mq, mk, mv, *, batch, tq):
    m = mq.shape[0]
    t = m // batch
    tq = min(tq, t)
    qk_blk = pl.BlockSpec((1, t, _QK_HEAD), lambda b, h: (b, 0, h))
    v_blk = pl.BlockSpec((1, t, MLA_V), lambda b, h: (b, 0, h))
    y = pl.pallas_call(
        functools.partial(_mla_prompt_kernel, tq=tq),
        grid=(batch, MLA_HEADS),
        in_specs=[qk_blk, qk_blk, v_blk],
        out_specs=v_blk,
        out_shape=jax.ShapeDtypeStruct((batch, t, MLA_HEADS * MLA_V), BF16),
        compiler_params=_params("parallel", "parallel"),
        name="mla_prompt",
    )(mq.reshape(batch, t, -1), mk.reshape(batch, t, -1), mv.reshape(batch, t, -1))
    return y.reshape(m, MLA_HEADS * MLA_V)


def _mla_qprep_kernel(mq_ref, wuk_ref, qlat_ref, qpe_ref):
    for hd in range(MLA_HEADS):
        q0 = hd * _QK_HEAD
        hs = slice(hd * LANES, (hd + 1) * LANES)
        qlat_ref[:, hs] = jnp.dot(mq_ref[:, q0:q0 + LANES], wuk_ref[hd], preferred_element_type=F32)
        qpe_ref[:, hs] = mq_ref[:, q0 + LANES:q0 + 2 * LANES].astype(F32)


def _mla_oproj_kernel(ol_ref, wuv_ref, y_ref):
    for hd in range(MLA_HEADS):
        hs = slice(hd * LANES, (hd + 1) * LANES)
        y_ref[:, hs] = jnp.dot(ol_ref[:, hs].astype(BF16), wuv_ref[hd],
                               preferred_element_type=F32).astype(BF16)


def _mla_decode_kernel(pt_ref, ql_ref, qp_ref, latn_ref, pen_ref, ckv_ref, cpe_ref,
                       o_ref, lat_buf, pe_buf, sems, *, page, n_pages):
    b = pl.program_id(0)
    nb = pl.num_programs(0)
    slot = b % 2

    def page_copies(bb, sl, p):
        pg = pt_ref[bb, p]
        rows = pl.ds(pl.multiple_of(p * page, page), page)
        return (pltpu.make_async_copy(ckv_ref.at[pg], lat_buf.at[sl, rows, :], sems.at[0, sl]),
                pltpu.make_async_copy(cpe_ref.at[pg], pe_buf.at[sl, rows, :], sems.at[1, sl]))

    def start_pages(bb, sl):
        def body(p, carry):
            for cp in page_copies(bb, sl, p):
                cp.start()
            return carry
        lax.fori_loop(0, n_pages, body, 0)

    def wait_pages(bb, sl):
        def body(p, carry):
            for cp in page_copies(bb, sl, p):
                cp.wait()
            return carry
        lax.fori_loop(0, n_pages, body, 0)

    @pl.when(b == 0)
    def _():
        start_pages(0, 0)

    @pl.when(b + 1 < nb)
    def _():
        start_pages(b + 1, 1 - slot)

    wait_pages(b, slot)

    pad = jnp.zeros((_Q_ROWS - MLA_HEADS, LANES), F32)
    ql = jnp.concatenate([ql_ref[0], pad], axis=0)
    qp = jnp.concatenate([qp_ref[0], pad], axis=0)[:, :MLA_ROPE]
    lat_b = lat_buf[slot].astype(BF16)
    pe_b = pe_buf[slot].astype(BF16)
    contract_last = (((1,), (1,)), ((), ()))
    s = (lax.dot_general(ql.astype(BF16), lat_b, contract_last, preferred_element_type=F32)
         + lax.dot_general(qp.astype(BF16), pe_b, contract_last, preferred_element_type=F32)) * MLA_SCALE
    lat_new = latn_ref[0]
    s_new = (jnp.sum(ql * lat_new, axis=1, keepdims=True)
             + jnp.sum(qp * pen_ref[0], axis=1, keepdims=True)) * MLA_SCALE
    mx = jnp.maximum(jnp.max(s, axis=1, keepdims=True), s_new)
    p = jnp.exp(s - mx)
    p_new = jnp.exp(s_new - mx)
    denom = jnp.sum(p, axis=1, keepdims=True) + p_new
    o = (jnp.dot(p.astype(BF16), lat_b, preferred_element_type=F32) + p_new * lat_new) / denom
    o_ref[0] = o[:MLA_HEADS]


def _mla_decode(mq, lat_new, pe_new, cache_kv, cache_pe, page_table, w_uk_t, w_uv_h):
    nb, n_pages = page_table.shape
    page = cache_kv.shape[1]
    past = n_pages * page
    hw = MLA_HEADS * LANES
    qlat, qpe = pl.pallas_call(
        _mla_qprep_kernel,
        out_shape=[jax.ShapeDtypeStruct((nb, hw), F32), jax.ShapeDtypeStruct((nb, hw), F32)],
        name="mla_qprep",
    )(mq, w_uk_t)
    per_b = lambda rows, width: pl.BlockSpec((1, rows, width), lambda b, pt: (b, 0, 0))
    grid_spec = pltpu.PrefetchScalarGridSpec(
        num_scalar_prefetch=1,
        grid=(nb,),
        in_specs=[per_b(MLA_HEADS, LANES), per_b(MLA_HEADS, LANES), per_b(1, MLA_KV_RANK), per_b(1, MLA_ROPE),
                  pl.BlockSpec(memory_space=pl.ANY), pl.BlockSpec(memory_space=pl.ANY)],
        out_specs=per_b(MLA_HEADS, MLA_KV_RANK),
        scratch_shapes=[pltpu.VMEM((2, past, MLA_KV_RANK), F32), pltpu.VMEM((2, past, MLA_ROPE), F32),
                        pltpu.SemaphoreType.DMA((2, 2))],
    )
    o_lat = pl.pallas_call(
        functools.partial(_mla_decode_kernel, page=page, n_pages=n_pages),
        grid_spec=grid_spec,
        out_shape=jax.ShapeDtypeStruct((nb, MLA_HEADS, MLA_KV_RANK), F32),
        compiler_params=_params("arbitrary"),
        name="mla_decode",
    )(page_table, qlat.reshape(nb, MLA_HEADS, LANES), qpe.reshape(nb, MLA_HEADS, LANES),
      lat_new.reshape(nb, 1, MLA_KV_RANK), pe_new.reshape(nb, 1, MLA_ROPE), cache_kv, cache_pe)
    return pl.pallas_call(
        _mla_oproj_kernel,
        out_shape=jax.ShapeDtypeStruct((nb, MLA_HEADS * MLA_V), BF16),
        name="mla_oproj",
    )(o_lat.reshape(nb, hw), w_uv_h)


def _rope_tables(pos):
    half = MLA_ROPE // 2
    inv_freq = jnp.power(jnp.float32(ROPE_THETA), -jnp.arange(half, dtype=F32) / half)
    ang = pos.astype(F32)[:, None] * inv_freq[None, :]
    cos, sin = jnp.cos(ang), jnp.sin(ang)
    z = jnp.zeros((pos.shape[0], LANES - MLA_ROPE), F32)
    return jnp.concatenate([cos, cos, z], axis=1), jnp.concatenate([sin, sin, z], axis=1)


def _rot_cols(w):
    half = MLA_ROPE // 2
    return jnp.concatenate([-w[:, half:], w[:, :half]], axis=1)


def _prep_mixer_weights(w_in, w_a_up, b_a, q_norm, w_uq, kv_norm, w_uk, w_uv):
    d = w_in.shape[0]
    gw, vw = GLA_HEADS * GLA_DK, GLA_HEADS * GLA_DV
    offs, o = [], 0
    for n in (gw, gw, vw, vw, GLA_GATE_RANK, MLA_Q_RANK, MLA_KV_RANK, MLA_ROPE):
        offs.append((o, o + n))
        o += n
    q, k, v, g, a_low, c_q, c_kv, k_pe = (w_in[:, a:b] for a, b in offs)
    zeros = lambda rows, n: jnp.zeros((rows, n), F32)
    z_half = zeros(d, LANES - MLA_ROPE)
    w_in_x = jnp.concatenate([q, k, v, g, c_q, c_kv, k_pe, z_half, _rot_cols(k_pe), z_half,
                              a_low, zeros(d, LANES - GLA_GATE_RANK)], axis=1).astype(BF16)
    w_a = jnp.concatenate([w_a_up, zeros(LANES - GLA_GATE_RANK, gw)], axis=0).astype(BF16)
    zq = zeros(MLA_Q_RANK, LANES - MLA_ROPE)
    cols = []
    for hd in range(MLA_HEADS):
        c0 = hd * (MLA_NOPE + MLA_ROPE)
        pe = w_uq[:, c0 + MLA_NOPE:c0 + MLA_NOPE + MLA_ROPE]
        cols += [w_uq[:, c0:c0 + MLA_NOPE], pe, zq, _rot_cols(pe), zq]
    w_uq_x = jnp.concatenate(cols, axis=1).astype(BF16)
    w_ukv = jnp.concatenate([w_uk.reshape(MLA_KV_RANK, -1), w_uv.reshape(MLA_KV_RANK, -1)], axis=1).astype(BF16)
    return {"w_in": w_in_x, "w_a": w_a, "b_a": b_a.reshape(1, gw), "q_norm": q_norm.reshape(1, -1),
            "w_uq": w_uq_x, "kv_norm": kv_norm.reshape(1, -1), "w_ukv": w_ukv}


def kernel(x_prompt, x_sample, cache_kv, cache_pe, state_gla, page_table, ffn1_norm_w, ffn1_w_gate, ffn1_w_up, ffn1_w_down, mix_norm_w, w_in, gla_w_a_up, gla_b_a, gla_norm_w, mla_q_norm_w, mla_w_uq, mla_kv_norm_w, mla_w_uk, mla_w_uv, w_out, ffn2_norm_w, ffn2_w_gate, ffn2_w_up, ffn2_w_down, final_norm_w):
    assert cache_kv.shape[0] == 1, "single-layer trunk"
    batch, seq, d = x_prompt.shape
    nb, dec_seq, _ = x_sample.shape
    assert dec_seq == 1
    page = cache_kv.shape[2]
    past = page_table.shape[1] * page

    f1 = (ffn1_w_gate[0].astype(BF16), ffn1_w_up[0].astype(BF16), ffn1_w_down[0].astype(BF16))
    f2 = (ffn2_w_gate[0].astype(BF16), ffn2_w_up[0].astype(BF16), ffn2_w_down[0].astype(BF16))
    wo = w_out[0].astype(BF16)
    mw = _prep_mixer_weights(w_in[0], gla_w_a_up[0], gla_b_a[0], mla_q_norm_w[0], mla_w_uq[0],
                             mla_kv_norm_w[0], mla_w_uk[0], mla_w_uv[0])
    w_uk_t = jnp.transpose(mla_w_uk[0], (1, 2, 0)).astype(BF16)
    w_uv_h = jnp.transpose(mla_w_uv[0], (1, 0, 2)).astype(BF16)

    tm = 256
    xp = x_prompt.reshape(batch * seq, d)
    xp = _ffn(xp, ffn1_norm_w[0], *f1, tm=tm)
    cos, sin = _rope_tables(jnp.arange(seq, dtype=jnp.int32))
    gq, gk, gla, gv, gg, mq, mk, mv, lat, kpe = _inproj(xp, mix_norm_w[0], mw, cos, sin, tm=tm)
    y_gla, s_prompt = _gla_prompt(gq, gk, gla, gv, gg, gla_norm_w[0], batch=batch, tc=512)
    y_mla = _mla_prompt(mq, mk, mv, batch=batch, tq=256)
    yp = _ffn(xp, ffn2_norm_w[0], *f2, mix=(y_gla, y_mla, wo), final_w=final_norm_w, tm=tm)

    xs = x_sample.reshape(nb, d)
    xs = _ffn(xs, ffn1_norm_w[0], *f1, tm=nb)
    cos_s, sin_s = _rope_tables(jnp.full((nb,), past, dtype=jnp.int32))
    sq, sk, sla, sv, sg, smq, _, _, slat, skpe = _inproj(xs, mix_norm_w[0], mw, cos_s, sin_s, tm=nb)
    ys_gla, s_sample = _gla_sample(sq, sk, sla, sv, sg, gla_norm_w[0], state_gla[0])
    ys_mla = _mla_decode(smq, slat, skpe, cache_kv[0], cache_pe[0], page_table, w_uk_t, w_uv_h)
    ys = _ffn(xs, ffn2_norm_w[0], *f2, mix=(ys_gla, ys_mla, wo), final_w=final_norm_w, tm=nb)

    return (yp.reshape(batch, seq, d), ys.reshape(nb, 1, d),
            lat.reshape(1, batch, seq, MLA_KV_RANK), kpe.reshape(1, batch, seq, MLA_ROPE),
            s_prompt[None], slat.reshape(1, nb, 1, MLA_KV_RANK), skpe.reshape(1, nb, 1, MLA_ROPE),
            s_sample[None])
```

```python
import functools
import math

import jax
import jax.numpy as jnp
from jax import lax
from jax.experimental import pallas as pl
from jax.experimental.pallas import tpu as pltpu

F32 = jnp.float32
BF16 = jnp.bfloat16

EPS = 1e-6
GLA_HEADS = 4
GLA_DK = 64
GLA_DV = 128
GLA_GATE_RANK = 16
GLA_GATE_NORM = 16.0
MLA_HEADS = 4
MLA_NOPE = 128
MLA_ROPE = 64
MLA_V = 128
MLA_Q_RANK = 256
MLA_KV_RANK = 128
ROPE_THETA = 10000.0
MLA_SCALE = 1.0 / math.sqrt(MLA_NOPE + MLA_ROPE)

LANES = 128
GLA_CHUNK = 128
GLA_PAIR = 2 * GLA_DK
VMEM_LIMIT = 56 * 1024 * 1024


def _rms(x, w):
    return x * lax.rsqrt(jnp.mean(x * x, axis=-1, keepdims=True) + EPS) * w


def _silu(x):
    return x * jax.nn.sigmoid(x)


def _const_spec(shape):
    nd = len(shape)
    return pl.BlockSpec(shape, lambda *_: (0,) * nd, pipeline_mode=pl.Buffered(1))


def _params(*sem):
    return pltpu.CompilerParams(dimension_semantics=sem, vmem_limit_bytes=VMEM_LIMIT)


def _ffn_kernel(*refs, has_mix, has_final):
    refs = list(refs)
    x_ref = refs.pop(0)
    if has_mix:
        ya_ref, yb_ref, wo_ref = refs.pop(0), refs.pop(0), refs.pop(0)
    nw_ref, wg_ref, wu_ref, wd_ref = refs.pop(0), refs.pop(0), refs.pop(0), refs.pop(0)
    if has_final:
        fw_ref = refs.pop(0)
    (o_ref,) = refs

    x = x_ref[...]
    if has_mix:
        y = jnp.concatenate([ya_ref[...], yb_ref[...]], axis=-1)
        x = x + jnp.dot(y, wo_ref[...], preferred_element_type=F32)
    h = _rms(x, nw_ref[...]).astype(BF16)
    g = jnp.dot(h, wg_ref[...], preferred_element_type=F32)
    u = jnp.dot(h, wu_ref[...], preferred_element_type=F32)
    a = (_silu(g) * u).astype(BF16)
    x = x + 0.5 * jnp.dot(a, wd_ref[...], preferred_element_type=F32)
    if has_final:
        x = _rms(x, fw_ref[...])
    o_ref[...] = x


def _ffn(x, norm_w, wg, wu, wd, *, mix=None, final_w=None, tm):
    m, d = x.shape
    ff = wg.shape[1]
    tm = min(tm, m)
    row = lambda i: (i, 0)
    args, specs = [x], [pl.BlockSpec((tm, d), row)]
    if mix is not None:
        ya, yb, wo = mix
        args += [ya, yb, wo]
        specs += [pl.BlockSpec((tm, ya.shape[1]), row), pl.BlockSpec((tm, yb.shape[1]), row),
                  _const_spec(wo.shape)]
    args += [norm_w.reshape(1, d), wg, wu, wd]
    specs += [_const_spec((1, d)), _const_spec((d, ff)), _const_spec((d, ff)), _const_spec((ff, d))]
    if final_w is not None:
        args.append(final_w.reshape(1, d))
        specs.append(_const_spec((1, d)))
    return pl.pallas_call(
        functools.partial(_ffn_kernel, has_mix=mix is not None, has_final=final_w is not None),
        grid=(m // tm,),
        in_specs=specs,
        out_specs=pl.BlockSpec((tm, d), row),
        out_shape=jax.ShapeDtypeStruct((m, d), F32),
        compiler_params=_params("parallel"),
        name="ffn_mix_final" if mix is not None else "ffn",
    )(*args)


_IN_GQ, _IN_GK, _IN_GV, _IN_GG, _IN_CQ, _IN_CKV, _IN_KPE, _IN_KROT, _IN_ALOW, _IN_END = (
    0, 256, 512, 1024, 1536, 1792, 1920, 2048, 2176, 2304)
_UQ_HEAD = 3 * LANES
_QK_HEAD = 2 * LANES
_Q_ROWS = 16


def _inproj_kernel(x_ref, nw_ref, win_ref, wa_ref, ba_ref, qn_ref, wuq_ref, kvn_ref, wukv_ref,
                   cos_ref, sin_ref,
                   gq_ref, gk_ref, gla_ref, gv_ref, gg_ref, mq_ref, mk_ref, mv_ref, lat_ref, kpe_ref):
    h = _rms(x_ref[...], nw_ref[...]).astype(BF16)
    p = jnp.dot(h, win_ref[...], preferred_element_type=F32)
    cos, sin = cos_ref[...], sin_ref[...]

    gq_ref[...] = p[:, _IN_GQ:_IN_GK] * (GLA_DK ** -0.5)
    gk_ref[...] = p[:, _IN_GK:_IN_GV]
    gv_ref[...] = p[:, _IN_GV:_IN_GG].astype(BF16)
    gg_ref[...] = p[:, _IN_GG:_IN_CQ]
    z = jnp.dot(p[:, _IN_ALOW:_IN_END].astype(BF16), wa_ref[...], preferred_element_type=F32) + ba_ref[...]
    log_sig = jnp.minimum(z, 0.0) - jnp.log1p(jnp.exp(-jnp.abs(z)))
    gla_ref[...] = log_sig * (1.0 / GLA_GATE_NORM)

    cq = _rms(p[:, _IN_CQ:_IN_CKV], qn_ref[...]).astype(BF16)
    qx = jnp.dot(cq, wuq_ref[...], preferred_element_type=F32)
    lat = _rms(p[:, _IN_CKV:_IN_KPE], kvn_ref[...])
    lat_ref[...] = lat
    kpe = p[:, _IN_KPE:_IN_KROT] * cos + p[:, _IN_KROT:_IN_ALOW] * sin
    kpe_ref[...] = kpe[:, :MLA_ROPE]
    kv = jnp.dot(lat.astype(BF16), wukv_ref[...], preferred_element_type=F32)
    kpe_b = kpe.astype(BF16)
    for hd in range(MLA_HEADS):
        q0 = hd * _UQ_HEAD
        q_pe = qx[:, q0 + LANES:q0 + 2 * LANES] * cos + qx[:, q0 + 2 * LANES:q0 + 3 * LANES] * sin
        o0 = hd * _QK_HEAD
        mq_ref[:, o0:o0 + LANES] = qx[:, q0:q0 + LANES].astype(BF16)
        mq_ref[:, o0 + LANES:o0 + 2 * LANES] = q_pe.astype(BF16)
        mk_ref[:, o0:o0 + LANES] = kv[:, hd * MLA_NOPE:(hd + 1) * MLA_NOPE].astype(BF16)
        mk_ref[:, o0 + LANES:o0 + 2 * LANES] = kpe_b
    mv_ref[...] = kv[:, MLA_HEADS * MLA_NOPE:].astype(BF16)


def _inproj(x, norm_w, w, cos, sin, *, tm):
    m, d = x.shape
    tm = min(tm, m)
    n_tab = cos.shape[0] // tm
    row = lambda i: (i, 0)
    tab = lambda i: (i % n_tab, 0)
    gw, mw = GLA_HEADS * GLA_DK, MLA_HEADS * _QK_HEAD
    outs = [((m, gw), F32), ((m, gw), F32), ((m, gw), F32), ((m, GLA_HEADS * GLA_DV), BF16),
            ((m, GLA_HEADS * GLA_DV), F32), ((m, mw), BF16), ((m, mw), BF16),
            ((m, MLA_HEADS * MLA_V), BF16), ((m, MLA_KV_RANK), F32), ((m, MLA_ROPE), F32)]
    consts = [norm_w.reshape(1, d), w["w_in"], w["w_a"], w["b_a"], w["q_norm"], w["w_uq"],
              w["kv_norm"], w["w_ukv"]]
    return pl.pallas_call(
        _inproj_kernel,
        grid=(m // tm,),
        in_specs=[pl.BlockSpec((tm, d), row)] + [_const_spec(c.shape) for c in consts]
        + [pl.BlockSpec((tm, LANES), tab), pl.BlockSpec((tm, LANES), tab)],
        out_specs=[pl.BlockSpec((tm, s[1]), row) for s, _ in outs],
        out_shape=[jax.ShapeDtypeStruct(s, dt) for s, dt in outs],
        compiler_params=_params("parallel"),
        name="inproj",
    )(x, *consts, cos, sin)


def _split3(x):
    hi = x.astype(BF16)
    r = x - hi.astype(F32)
    mid = r.astype(BF16)
    lo = (r - mid.astype(F32)).astype(BF16)
    return hi, mid, lo


def _gla_prompt_kernel(q_ref, k_ref, la_ref, v_ref, g_ref, gn_ref, y_ref, s_ref):
    c_sz = GLA_CHUNK

    @pl.when(pl.program_id(1) == 0)
    def _():
        s_ref[...] = jnp.zeros_like(s_ref)

    row = lax.broadcasted_iota(jnp.int32, (c_sz, c_sz), 0)
    col = lax.broadcasted_iota(jnp.int32, (c_sz, c_sz), 1)
    tri = row >= col
    tri_b = jnp.where(tri, 1.0, 0.0).astype(BF16)
    lane_lo = col < GLA_DK
    row_lo = row < GLA_DK
    gn = gn_ref[...]

    def chunk(c, carry):
        r0 = pl.multiple_of(c * c_sz, c_sz)
        rows = pl.ds(r0, c_sz)
        la = la_ref[0, rows, :]
        b = sum(jnp.dot(tri_b, t, preferred_element_type=F32) for t in _split3(la))
        e_pos = jnp.exp(b)
        qp = q_ref[0, rows, :] * e_pos
        kp = k_ref[0, rows, :] * jnp.exp(-b)
        kd = kp * e_pos[c_sz - 1:c_sz, :]
        e_last_col = jnp.exp(jnp.sum(la.T, axis=1, keepdims=True))
        v = v_ref[0, rows, :]
        g = g_ref[0, rows, :]
        for pr in range(GLA_HEADS // 2):
            sl = slice(pr * GLA_PAIR, (pr + 1) * GLA_PAIR)
            qp_p = qp[:, sl]
            q_heads = (jnp.where(lane_lo, qp_p, 0.0).astype(BF16), jnp.where(lane_lo, 0.0, qp_p).astype(BF16))
            attn = lax.dot_general(jnp.concatenate(q_heads, axis=0), kp[:, sl].astype(BF16),
                                   (((1,), (1,)), ((), ())), preferred_element_type=F32)
            s_old = s_ref[0, pr]
            s_old_b = s_old.astype(BF16)
            for j in range(2):
                hd = 2 * pr + j
                hs = slice(hd * GLA_DV, (hd + 1) * GLA_DV)
                a_j = jnp.where(tri, attn[j * c_sz:(j + 1) * c_sz], 0.0).astype(BF16)
                o = jnp.dot(jnp.concatenate([a_j, q_heads[j]], axis=1),
                            jnp.concatenate([v[:, hs], s_old_b], axis=0), preferred_element_type=F32)
                y_ref[0, rows, hs] = (_rms(o, gn) * _silu(g[:, hs])).astype(BF16)
            kd_t = kd[:, sl].T
            kd_heads = jnp.concatenate([jnp.where(row_lo, kd_t, 0.0).astype(BF16),
                                        jnp.where(row_lo, 0.0, kd_t).astype(BF16)], axis=1)
            v_pair = jnp.concatenate([v[:, (2 * pr) * GLA_DV:(2 * pr + 1) * GLA_DV],
                                      v[:, (2 * pr + 1) * GLA_DV:(2 * pr + 2) * GLA_DV]], axis=0)
            s_ref[0, pr] = e_last_col[sl] * s_old + jnp.dot(kd_heads, v_pair, preferred_element_type=F32)
        return carry

    lax.fori_loop(0, q_ref.shape[1] // c_sz, chunk, 0, unroll=True)


def _gla_prompt(gq, gk, gla, gv, gg, g_norm, *, batch, tc):
    m = gq.shape[0]
    t = m // batch
    tc = min(tc, t)
    r3 = lambda a: a.reshape(batch, t, a.shape[1])
    blk = lambda width: pl.BlockSpec((1, tc, width), lambda b, i: (b, i, 0))
    kw, vw = GLA_HEADS * GLA_DK, GLA_HEADS * GLA_DV
    y, s = pl.pallas_call(
        _gla_prompt_kernel,
        grid=(batch, t // tc),
        in_specs=[blk(kw), blk(kw), blk(kw), blk(vw), blk(vw), _const_spec((1, GLA_DV))],
        out_specs=[blk(vw), pl.BlockSpec((1, GLA_HEADS // 2, GLA_PAIR, GLA_DV), lambda b, i: (b, 0, 0, 0))],
        out_shape=[jax.ShapeDtypeStruct((batch, t, vw), BF16),
                   jax.ShapeDtypeStruct((batch, GLA_HEADS // 2, GLA_PAIR, GLA_DV), F32)],
        compiler_params=_params("parallel", "arbitrary"),
        name="gla_prompt",
    )(r3(gq), r3(gk), r3(gla), r3(gv), r3(gg), g_norm.reshape(1, GLA_DV))
    return y.reshape(m, vw), s.reshape(batch, GLA_HEADS, GLA_DK, GLA_DV)


def _gla_sample_kernel(q_ref, k_ref, la_ref, v_ref, g_ref, gn_ref, s_ref, y_ref, so_ref):
    nb = q_ref.shape[0]
    q, k = q_ref[...], k_ref[...]
    a = jnp.exp(la_ref[...])
    qa = q * a
    v = v_ref[...].astype(F32)
    g = g_ref[...]
    lane_lo = lax.broadcasted_iota(jnp.int32, (nb, GLA_PAIR), 1) < GLA_DK
    qk = q * k
    qk_heads = (jnp.sum(jnp.where(lane_lo, qk, 0.0), axis=1, keepdims=True),
                jnp.sum(jnp.where(lane_lo, 0.0, qk), axis=1, keepdims=True))
    o = [qk_heads[j] * v[:, j * GLA_DV:(j + 1) * GLA_DV] for j in range(2)]
    for c in range(GLA_PAIR):
        j = c // GLA_DK
        s_c = s_ref[:, c, :]
        o[j] = o[j] + qa[:, c:c + 1] * s_c
        so_ref[:, c, :] = a[:, c:c + 1] * s_c + k[:, c:c + 1] * v[:, j * GLA_DV:(j + 1) * GLA_DV]
    gn = gn_ref[...]
    for j in range(2):
        hs = slice(j * GLA_DV, (j + 1) * GLA_DV)
        y_ref[:, hs] = (_rms(o[j], gn) * _silu(g[:, hs])).astype(BF16)


def _gla_sample(gq, gk, gla, gv, gg, g_norm, state):
    nb = gq.shape[0]
    kw, vw = GLA_HEADS * GLA_DK, GLA_HEADS * GLA_DV
    st = state.reshape(nb, kw, GLA_DV)
    lane_blk = lambda width: pl.BlockSpec((nb, width), lambda p: (0, p))
    st_blk = pl.BlockSpec((nb, GLA_PAIR, GLA_DV), lambda p: (0, p, 0))
    y, s = pl.pallas_call(
        _gla_sample_kernel,
        grid=(GLA_HEADS // 2,),
        in_specs=[lane_blk(GLA_PAIR), lane_blk(GLA_PAIR), lane_blk(GLA_PAIR), lane_blk(2 * GLA_DV),
                  lane_blk(2 * GLA_DV), _const_spec((1, GLA_DV)), st_blk],
        out_specs=[lane_blk(2 * GLA_DV), st_blk],
        out_shape=[jax.ShapeDtypeStruct((nb, vw), BF16), jax.ShapeDtypeStruct((nb, kw, GLA_DV), F32)],
        compiler_params=_params("parallel"),
        name="gla_sample",
    )(gq, gk, gla, gv, gg, g_norm.reshape(1, GLA_DV), st)
    return y, s.reshape(nb, GLA_HEADS, GLA_DK, GLA_DV)


def _mla_prompt_kernel(q_ref, k_ref, v_ref, o_ref, *, tq):
    t = q_ref.shape[1]
    for qi in range(t // tq):
        q0, kend = qi * tq, (qi + 1) * tq
        s = lax.dot_general(q_ref[0, q0:kend, :], k_ref[0, 0:kend, :], (((1,), (1,)), ((), ())),
                            preferred_element_type=F32) * MLA_SCALE
        row = lax.broadcasted_iota(jnp.int32, (tq, kend), 0) + q0
        col = lax.broadcasted_iota(jnp.int32, (tq, kend), 1)
        s = jnp.where(col <= row, s, -jnp.inf)
        p = jnp.exp(s - jnp.max(s, axis=1, keepdims=True))
        denom = jnp.sum(p, axis=1, keepdims=True)
        o = jnp.dot(p.astype(BF16), v_ref[0, 0:kend, :], preferred_element_type=F32)
        o_ref[0, q0:kend, :] = (o / denom).astype(BF16)


def _mla_prompt(mq, mk, mv, *, batch, tq):
    m = mq.shape[0]
    t = m // batch
    tq = min(tq, t)
    qk_blk = pl.BlockSpec((1, t, _QK_HEAD), lambda b, h: (b, 0, h))
    v_blk = pl.BlockSpec((1, t, MLA_V), lambda b, h: (b, 0, h))
    y = pl.pallas_call(
        functools.partial(_mla_prompt_kernel, tq=tq),
        grid=(batch, MLA_HEADS),
        in_specs=[qk_blk, qk_blk, v_blk],
        out_specs=v_blk,
        out_shape=jax.ShapeDtypeStruct((batch, t, MLA_HEADS * MLA_V), BF16),
        compiler_params=_params("parallel", "parallel"),
        name="mla_prompt",
    )(mq.reshape(batch, t, -1), mk.reshape(batch, t, -1), mv.reshape(batch, t, -1))
    return y.reshape(m, MLA_HEADS * MLA_V)


def _mla_qprep_kernel(mq_ref, wuk_ref, qlat_ref, qpe_ref):
    for hd in range(MLA_HEADS):
        q0 = hd * _QK_HEAD
        hs = slice(hd * LANES, (hd + 1) * LANES)
        qlat_ref[:, hs] = jnp.dot(mq_ref[:, q0:q0 + LANES], wuk_ref[hd], preferred_element_type=F32)
        qpe_ref[:, hs] = mq_ref[:, q0 + LANES:q0 + 2 * LANES].astype(F32)


def _mla_oproj_kernel(ol_ref, wuv_ref, y_ref):
    for hd in range(MLA_HEADS):
        hs = slice(hd * LANES, (hd + 1) * LANES)
        y_ref[:, hs] = jnp.dot(ol_ref[:, hs].astype(BF16), wuv_ref[hd],
                               preferred_element_type=F32).astype(BF16)


def _mla_decode_kernel(pt_ref, ql_ref, qp_ref, latn_ref, pen_ref, ckv_ref, cpe_ref,
                       o_ref, lat_buf, pe_buf, sems, *, page, n_pages):
    b = pl.program_id(0)
    nb = pl.num_programs(0)
    slot = b % 2

    def page_copies(bb, sl, p):
        pg = pt_ref[bb, p]
        rows = pl.ds(pl.multiple_of(p * page, page), page)
        return (pltpu.make_async_copy(ckv_ref.at[pg], lat_buf.at[sl, rows, :], sems.at[0, sl]),
                pltpu.make_async_copy(cpe_ref.at[pg], pe_buf.at[sl, :, rows], sems.at[1, sl]))

    def start_pages(bb, sl):
        def body(p, carry):
            for cp in page_copies(bb, sl, p):
                cp.start()
            return carry
        lax.fori_loop(0, n_pages, body, 0, unroll=8)

    def wait_pages(sl):
        pltpu.make_async_copy(lat_buf.at[sl], lat_buf.at[sl], sems.at[0, sl]).wait()
        pltpu.make_async_copy(pe_buf.at[sl], pe_buf.at[sl], sems.at[1, sl]).wait()

    @pl.when(b == 0)
    def _():
        start_pages(0, 0)

    @pl.when(b + 1 < nb)
    def _():
        start_pages(b + 1, 1 - slot)

    wait_pages(slot)

    pad = jnp.zeros((_Q_ROWS - MLA_HEADS, LANES), F32)
    ql = jnp.concatenate([ql_ref[0], pad], axis=0)
    qp = jnp.concatenate([qp_ref[0], pad], axis=0)[:, :MLA_ROPE]
    lat_b = lat_buf[slot]
    pe_b = pe_buf[slot]
    s = (lax.dot_general(ql, lat_b, (((1,), (1,)), ((), ())), preferred_element_type=F32)
         + jnp.dot(qp, pe_b, preferred_element_type=F32)) * MLA_SCALE
    lat_new = latn_ref[0]
    s_new = (jnp.sum(ql * lat_new, axis=1, keepdims=True)
             + jnp.sum(qp * pen_ref[0], axis=1, keepdims=True)) * MLA_SCALE
    mx = jnp.maximum(jnp.max(s, axis=1, keepdims=True), s_new)
    p = jnp.exp(s - mx)
    p_new = jnp.exp(s_new - mx)
    denom = jnp.sum(p, axis=1, keepdims=True) + p_new
    o = (jnp.dot(p, lat_b, preferred_element_type=F32) + p_new * lat_new) / denom
    o_ref[0] = o[:MLA_HEADS]


def _mla_decode(mq, lat_new, pe_new, cache_kv, cache_pe, page_table, w_uk_t, w_uv_h):
    nb, n_pages = page_table.shape
    page = cache_kv.shape[1]
    past = n_pages * page
    hw = MLA_HEADS * LANES
    qlat, qpe = pl.pallas_call(
        _mla_qprep_kernel,
        out_shape=[jax.ShapeDtypeStruct((nb, hw), F32), jax.ShapeDtypeStruct((nb, hw), F32)],
        name="mla_qprep",
    )(mq, w_uk_t)
    per_b = lambda rows, width: pl.BlockSpec((1, rows, width), lambda b, pt: (b, 0, 0))
    grid_spec = pltpu.PrefetchScalarGridSpec(
        num_scalar_prefetch=1,
        grid=(nb,),
        in_specs=[per_b(MLA_HEADS, LANES), per_b(MLA_HEADS, LANES), per_b(1, MLA_KV_RANK), per_b(1, MLA_ROPE),
                  pl.BlockSpec(memory_space=pl.ANY), pl.BlockSpec(memory_space=pl.ANY)],
        out_specs=per_b(MLA_HEADS, MLA_KV_RANK),
        scratch_shapes=[pltpu.VMEM((2, past, MLA_KV_RANK), F32), pltpu.VMEM((2, MLA_ROPE, past), F32),
                        pltpu.SemaphoreType.DMA((2, 2))],
    )
    o_lat = pl.pallas_call(
        functools.partial(_mla_decode_kernel, page=page, n_pages=n_pages),
        grid_spec=grid_spec,
        out_shape=jax.ShapeDtypeStruct((nb, MLA_HEADS, MLA_KV_RANK), F32),
        compiler_params=_params("arbitrary"),
        name="mla_decode",
    )(page_table, qlat.reshape(nb, MLA_HEADS, LANES), qpe.reshape(nb, MLA_HEADS, LANES),
      lat_new.reshape(nb, 1, MLA_KV_RANK), pe_new.reshape(nb, 1, MLA_ROPE), cache_kv, cache_pe)
    return pl.pallas_call(
        _mla_oproj_kernel,
        out_shape=jax.ShapeDtypeStruct((nb, MLA_HEADS * MLA_V), BF16),
        name="mla_oproj",
    )(o_lat.reshape(nb, hw), w_uv_h)


def _rope_tables(pos):
    half = MLA_ROPE // 2
    inv_freq = jnp.power(jnp.float32(ROPE_THETA), -jnp.arange(half, dtype=F32) / half)
    ang = pos.astype(F32)[:, None] * inv_freq[None, :]
    cos, sin = jnp.cos(ang), jnp.sin(ang)
    z = jnp.zeros((pos.shape[0], LANES - MLA_ROPE), F32)
    return jnp.concatenate([cos, cos, z], axis=1), jnp.concatenate([sin, sin, z], axis=1)


def _rot_cols(w):
    half = MLA_ROPE // 2
    return jnp.concatenate([-w[:, half:], w[:, :half]], axis=1)


def _prep_mixer_weights(w_in, w_a_up, b_a, q_norm, w_uq, kv_norm, w_uk, w_uv):
    d = w_in.shape[0]
    gw, vw = GLA_HEADS * GLA_DK, GLA_HEADS * GLA_DV
    offs, o = [], 0
    for n in (gw, gw, vw, vw, GLA_GATE_RANK, MLA_Q_RANK, MLA_KV_RANK, MLA_ROPE):
        offs.append((o, o + n))
        o += n
    q, k, v, g, a_low, c_q, c_kv, k_pe = (w_in[:, a:b] for a, b in offs)
    zeros = lambda rows, n: jnp.zeros((rows, n), F32)
    z_half = zeros(d, LANES - MLA_ROPE)
    w_in_x = jnp.concatenate([q, k, v, g, c_q, c_kv, k_pe, z_half, _rot_cols(k_pe), z_half,
                              a_low, zeros(d, LANES - GLA_GATE_RANK)], axis=1).astype(BF16)
    w_a = jnp.concatenate([w_a_up, zeros(LANES - GLA_GATE_RANK, gw)], axis=0).astype(BF16)
    zq = zeros(MLA_Q_RANK, LANES - MLA_ROPE)
    cols = []
    for hd in range(MLA_HEADS):
        c0 = hd * (MLA_NOPE + MLA_ROPE)
        pe = w_uq[:, c0 + MLA_NOPE:c0 + MLA_NOPE + MLA_ROPE]
        cols += [w_uq[:, c0:c0 + MLA_NOPE], pe, zq, _rot_cols(pe), zq]
    w_uq_x = jnp.concatenate(cols, axis=1).astype(BF16)
    w_ukv = jnp.concatenate([w_uk.reshape(MLA_KV_RANK, -1), w_uv.reshape(MLA_KV_RANK, -1)], axis=1).astype(BF16)
    return {"w_in": w_in_x, "w_a": w_a, "b_a": b_a.reshape(1, gw), "q_norm": q_norm.reshape(1, -1),
            "w_uq": w_uq_x, "kv_norm": kv_norm.reshape(1, -1), "w_ukv": w_ukv}


def kernel(x_prompt, x_sample, cache_kv, cache_pe, state_gla, page_table, ffn1_norm_w, ffn1_w_gate, ffn1_w_up, ffn1_w_down, mix_norm_w, w_in, gla_w_a_up, gla_b_a, gla_norm_w, mla_q_norm_w, mla_w_uq, mla_kv_norm_w, mla_w_uk, mla_w_uv, w_out, ffn2_norm_w, ffn2_w_gate, ffn2_w_up, ffn2_w_down, final_norm_w):
    assert cache_kv.shape[0] == 1, "single-layer trunk"
    batch, seq, d = x_prompt.shape
    nb, dec_seq, _ = x_sample.shape
    assert dec_seq == 1
    page = cache_kv.shape[2]
    past = page_table.shape[1] * page

    f1 = (ffn1_w_gate[0].astype(BF16), ffn1_w_up[0].astype(BF16), ffn1_w_down[0].astype(BF16))
    f2 = (ffn2_w_gate[0].astype(BF16), ffn2_w_up[0].astype(BF16), ffn2_w_down[0].astype(BF16))
    wo = w_out[0].astype(BF16)
    mw = _prep_mixer_weights(w_in[0], gla_w_a_up[0], gla_b_a[0], mla_q_norm_w[0], mla_w_uq[0],
                             mla_kv_norm_w[0], mla_w_uk[0], mla_w_uv[0])
    w_uk_t = jnp.transpose(mla_w_uk[0], (1, 2, 0)).astype(BF16)
    w_uv_h = jnp.transpose(mla_w_uv[0], (1, 0, 2)).astype(BF16)

    tm, tm_ffn = 256, 512
    xp = x_prompt.reshape(batch * seq, d)
    xp = _ffn(xp, ffn1_norm_w[0], *f1, tm=tm_ffn)
    cos, sin = _rope_tables(jnp.arange(seq, dtype=jnp.int32))
    gq, gk, gla, gv, gg, mq, mk, mv, lat, kpe = _inproj(xp, mix_norm_w[0], mw, cos, sin, tm=tm)
    y_gla, s_prompt = _gla_prompt(gq, gk, gla, gv, gg, gla_norm_w[0], batch=batch, tc=512)
    y_mla = _mla_prompt(mq, mk, mv, batch=batch, tq=256)
    yp = _ffn(xp, ffn2_norm_w[0], *f2, mix=(y_gla, y_mla, wo), final_w=final_norm_w, tm=tm_ffn)

    xs = x_sample.reshape(nb, d)
    xs = _ffn(xs, ffn1_norm_w[0], *f1, tm=nb)
    cos_s, sin_s = _rope_tables(jnp.full((nb,), past, dtype=jnp.int32))
    sq, sk, sla, sv, sg, smq, _, _, slat, skpe = _inproj(xs, mix_norm_w[0], mw, cos_s, sin_s, tm=nb)
    ys_gla, s_sample = _gla_sample(sq, sk, sla, sv, sg, gla_norm_w[0], state_gla[0])
    ys_mla = _mla_decode(smq, slat, skpe, cache_kv[0], jnp.swapaxes(cache_pe[0], 1, 2), page_table,
                         w_uk_t, w_uv_h)
    ys = _ffn(xs, ffn2_norm_w[0], *f2, mix=(ys_gla, ys_mla, wo), final_w=final_norm_w, tm=nb)

    return (yp.reshape(batch, seq, d), ys.reshape(nb, 1, d),
            lat.reshape(1, batch, seq, MLA_KV_RANK), kpe.reshape(1, batch, seq, MLA_ROPE),
            s_prompt[None], slat.reshape(1, nb, 1, MLA_KV_RANK), skpe.reshape(1, nb, 1, MLA_ROPE),
            s_sample[None])
```

```python
import functools
import math

import jax
import jax.numpy as jnp
from jax import lax
from jax.experimental import pallas as pl
from jax.experimental.pallas import tpu as pltpu

F32 = jnp.float32
BF16 = jnp.bfloat16

EPS = 1e-6
GLA_HEADS = 4
GLA_DK = 64
GLA_DV = 128
GLA_GATE_RANK = 16
GLA_GATE_NORM = 16.0
MLA_HEADS = 4
MLA_NOPE = 128
MLA_ROPE = 64
MLA_V = 128
MLA_Q_RANK = 256
MLA_KV_RANK = 128
ROPE_THETA = 10000.0
MLA_SCALE = 1.0 / math.sqrt(MLA_NOPE + MLA_ROPE)
MLA_Q_SCALE = MLA_SCALE * math.log2(math.e)

LANES = 128
GLA_CHUNK = 128
GLA_PAIR = 2 * GLA_DK
GLA_MAX_CHUNK_DECAY = 60.0
VMEM_LIMIT = 56 * 1024 * 1024


def _rms(x, w):
    return x * lax.rsqrt(jnp.mean(x * x, axis=-1, keepdims=True) + EPS) * w


def _silu(x):
    return x * jax.nn.sigmoid(x)


def _const_spec(shape):
    nd = len(shape)
    return pl.BlockSpec(shape, lambda *_: (0,) * nd, pipeline_mode=pl.Buffered(1))


def _params(*sem):
    return pltpu.CompilerParams(dimension_semantics=sem, vmem_limit_bytes=VMEM_LIMIT)


def _ffn_kernel(*refs, has_mix, has_final):
    refs = list(refs)
    x_ref = refs.pop(0)
    if has_mix:
        ya_ref, yb_ref, wo_ref = refs.pop(0), refs.pop(0), refs.pop(0)
    nw_ref, wg_ref, wu_ref, wd_ref = refs.pop(0), refs.pop(0), refs.pop(0), refs.pop(0)
    if has_final:
        fw_ref = refs.pop(0)
    (o_ref,) = refs

    x = x_ref[...]
    if has_mix:
        y = jnp.concatenate([ya_ref[...], yb_ref[...]], axis=-1)
        x = x + jnp.dot(y, wo_ref[...], preferred_element_type=F32)
    h = _rms(x, nw_ref[...]).astype(BF16)
    g = jnp.dot(h, wg_ref[...], preferred_element_type=F32)
    u = jnp.dot(h, wu_ref[...], preferred_element_type=F32)
    a = (_silu(g) * u).astype(BF16)
    x = x + 0.5 * jnp.dot(a, wd_ref[...], preferred_element_type=F32)
    if has_final:
        x = _rms(x, fw_ref[...])
    o_ref[...] = x


def _ffn(x, norm_w, wg, wu, wd, *, mix=None, final_w=None, tm):
    m, d = x.shape
    ff = wg.shape[1]
    tm = min(tm, m)
    row = lambda i: (i, 0)
    args, specs = [x], [pl.BlockSpec((tm, d), row)]
    if mix is not None:
        ya, yb, wo = mix
        args += [ya, yb, wo]
        specs += [pl.BlockSpec((tm, ya.shape[1]), row), pl.BlockSpec((tm, yb.shape[1]), row),
                  _const_spec(wo.shape)]
    args += [norm_w.reshape(1, d), wg, wu, wd]
    specs += [_const_spec((1, d)), _const_spec((d, ff)), _const_spec((d, ff)), _const_spec((ff, d))]
    if final_w is not None:
        args.append(final_w.reshape(1, d))
        specs.append(_const_spec((1, d)))
    return pl.pallas_call(
        functools.partial(_ffn_kernel, has_mix=mix is not None, has_final=final_w is not None),
        grid=(m // tm,),
        in_specs=specs,
        out_specs=pl.BlockSpec((tm, d), row),
        out_shape=jax.ShapeDtypeStruct((m, d), F32),
        compiler_params=_params("parallel"),
        name="ffn_mix_final" if mix is not None else "ffn",
    )(*args)


_IN_GQ, _IN_GK, _IN_GV, _IN_GG, _IN_CQ, _IN_CKV, _IN_KPE, _IN_KROT, _IN_ALOW, _IN_END = (
    0, 256, 512, 1024, 1536, 1792, 1920, 2048, 2176, 2304)
_UQ_HEAD = 3 * LANES
_QK_HEAD = 2 * LANES
_Q_ROWS = 16


def _inproj_kernel(x_ref, nw_ref, win_ref, wa_ref, ba_ref, qn_ref, wuq_ref, kvn_ref, wukv_ref,
                   cos_ref, sin_ref,
                   gq_ref, gk_ref, gla_ref, gv_ref, gg_ref, mq_ref, mk_ref, mv_ref, lat_ref, kpe_ref):
    h = _rms(x_ref[...], nw_ref[...]).astype(BF16)
    p = jnp.dot(h, win_ref[...], preferred_element_type=F32)
    cos, sin = cos_ref[...], sin_ref[...]

    gq_ref[...] = p[:, _IN_GQ:_IN_GK] * (GLA_DK ** -0.5)
    gk_ref[...] = p[:, _IN_GK:_IN_GV]
    gv_ref[...] = p[:, _IN_GV:_IN_GG].astype(BF16)
    gg_ref[...] = p[:, _IN_GG:_IN_CQ]
    z = jnp.dot(p[:, _IN_ALOW:_IN_END].astype(BF16), wa_ref[...], preferred_element_type=F32) + ba_ref[...]
    log_sig = jnp.minimum(z, 0.0) - jnp.log1p(jnp.exp(-jnp.abs(z)))
    gla_ref[...] = log_sig * (1.0 / GLA_GATE_NORM)

    cq = _rms(p[:, _IN_CQ:_IN_CKV], qn_ref[...]).astype(BF16)
    qx = jnp.dot(cq, wuq_ref[...], preferred_element_type=F32)
    lat = _rms(p[:, _IN_CKV:_IN_KPE], kvn_ref[...])
    lat_ref[...] = lat
    kpe = p[:, _IN_KPE:_IN_KROT] * cos + p[:, _IN_KROT:_IN_ALOW] * sin
    kpe_ref[...] = kpe[:, :MLA_ROPE]
    kv = jnp.dot(lat.astype(BF16), wukv_ref[...], preferred_element_type=F32)
    kpe_b = kpe.astype(BF16)
    for hd in range(MLA_HEADS):
        q0 = hd * _UQ_HEAD
        q_pe = qx[:, q0 + LANES:q0 + 2 * LANES] * cos + qx[:, q0 + 2 * LANES:q0 + 3 * LANES] * sin
        o0 = hd * _QK_HEAD
        mq_ref[:, o0:o0 + LANES] = (qx[:, q0:q0 + LANES] * MLA_Q_SCALE).astype(BF16)
        mq_ref[:, o0 + LANES:o0 + 2 * LANES] = (q_pe * MLA_Q_SCALE).astype(BF16)
        mk_ref[:, o0:o0 + LANES] = kv[:, hd * MLA_NOPE:(hd + 1) * MLA_NOPE].astype(BF16)
        mk_ref[:, o0 + LANES:o0 + 2 * LANES] = kpe_b
    mv_ref[...] = kv[:, MLA_HEADS * MLA_NOPE:].astype(BF16)


def _inproj(x, norm_w, w, cos, sin, *, tm):
    m, d = x.shape
    tm = min(tm, m)
    n_tab = cos.shape[0] // tm
    row = lambda i: (i, 0)
    tab = lambda i: (i % n_tab, 0)
    gw, mw = GLA_HEADS * GLA_DK, MLA_HEADS * _QK_HEAD
    outs = [((m, gw), F32), ((m, gw), F32), ((m, gw), F32), ((m, GLA_HEADS * GLA_DV), BF16),
            ((m, GLA_HEADS * GLA_DV), F32), ((m, mw), BF16), ((m, mw), BF16),
            ((m, MLA_HEADS * MLA_V), BF16), ((m, MLA_KV_RANK), F32), ((m, MLA_ROPE), F32)]
    consts = [norm_w.reshape(1, d), w["w_in"], w["w_a"], w["b_a"], w["q_norm"], w["w_uq"],
              w["kv_norm"], w["w_ukv"]]
    out_specs = [pl.BlockSpec((tm, s[1]), row) for s, _ in outs]
    return pl.pallas_call(
        _inproj_kernel,
        grid=(m // tm,),
        in_specs=[pl.BlockSpec((tm, d), row)] + [_const_spec(c.shape) for c in consts]
        + [pl.BlockSpec((tm, LANES), tab), pl.BlockSpec((tm, LANES), tab)],
        out_specs=out_specs,
        out_shape=[jax.ShapeDtypeStruct(s, dt) for s, dt in outs],
        compiler_params=_params("parallel"),
        name="inproj",
    )(x, *consts, cos, sin)


def _split3(x):
    hi = x.astype(BF16)
    r = x - hi.astype(F32)
    mid = r.astype(BF16)
    lo = (r - mid.astype(F32)).astype(BF16)
    return hi, mid, lo


def _gla_prompt_kernel(q_ref, k_ref, la_ref, v_ref, g_ref, gn_ref, y_ref, s_ref, o_scr):
    c_sz = GLA_CHUNK
    n_chunks = q_ref.shape[1] // c_sz

    @pl.when(pl.program_id(1) == 0)
    def _():
        s_ref[...] = jnp.zeros_like(s_ref)

    row = lax.broadcasted_iota(jnp.int32, (c_sz, c_sz), 0)
    col = lax.broadcasted_iota(jnp.int32, (c_sz, c_sz), 1)
    tri = row >= col
    tri_b = jnp.where(tri, 1.0, 0.0).astype(BF16)
    lane_lo = col < GLA_DK
    row_lo = row < GLA_DK
    gn = gn_ref[...]

    def finish(rows, hd, o, g):
        hs = slice(hd * GLA_DV, (hd + 1) * GLA_DV)
        y_ref[0, rows, hs] = (_rms(o, gn) * _silu(g[:, hs])).astype(BF16)

    def exact_chunk(c, carry):
        rows = pl.ds(pl.multiple_of(c * c_sz, c_sz), c_sz)
        a_t = jnp.exp(la_ref[0, rows, :]).T
        k_t = k_ref[0, rows, :].T
        q_t = q_ref[0, rows, :].T
        v = v_ref[0, rows, :].astype(F32)
        tok_lane = lax.broadcasted_iota(jnp.int32, a_t.shape, 1)
        tok_row = lax.broadcasted_iota(jnp.int32, v.shape, 0)
        o_scr[...] = jnp.zeros_like(o_scr)

        def token(t, carry_t):
            pick = tok_lane == t
            a_c = jnp.sum(jnp.where(pick, a_t, 0.0), axis=1, keepdims=True)
            k_c = jnp.sum(jnp.where(pick, k_t, 0.0), axis=1, keepdims=True)
            q_c = jnp.sum(jnp.where(pick, q_t, 0.0), axis=1, keepdims=True)
            v_r = jnp.sum(jnp.where(tok_row == t, v, 0.0), axis=0, keepdims=True)
            o_heads = []
            for pr in range(GLA_HEADS // 2):
                sl = slice(pr * GLA_PAIR, (pr + 1) * GLA_PAIR)
                v_lo = v_r[:, (2 * pr) * GLA_DV:(2 * pr + 1) * GLA_DV]
                v_hi = v_r[:, (2 * pr + 1) * GLA_DV:(2 * pr + 2) * GLA_DV]
                s_new = a_c[sl] * s_ref[0, pr] + k_c[sl] * jnp.where(row_lo, v_lo, v_hi)
                s_ref[0, pr] = s_new
                qs = q_c[sl] * s_new
                o_heads.append(jnp.sum(jnp.where(row_lo, qs, 0.0), axis=0, keepdims=True))
                o_heads.append(jnp.sum(jnp.where(row_lo, 0.0, qs), axis=0, keepdims=True))
            o_scr[...] = jnp.where(tok_row == t, jnp.concatenate(o_heads, axis=1), o_scr[...])
            return carry_t

        lax.fori_loop(0, c_sz, token, 0)
        g = g_ref[0, rows, :]
        for hd in range(GLA_HEADS):
            finish(rows, hd, o_scr[:, hd * GLA_DV:(hd + 1) * GLA_DV], g)
        return carry

    def chunk(c, carry):
        r0 = pl.multiple_of(c * c_sz, c_sz)
        rows = pl.ds(r0, c_sz)
        la = la_ref[0, rows, :]
        b = sum(jnp.dot(tri_b, t, preferred_element_type=F32) for t in _split3(la))
        e_pos = jnp.exp(b)
        qp = q_ref[0, rows, :] * e_pos
        kp = k_ref[0, rows, :] * jnp.exp(-b)
        kd = kp * e_pos[c_sz - 1:c_sz, :]
        e_last_col = jnp.exp(jnp.sum(la.T, axis=1, keepdims=True))
        v = v_ref[0, rows, :]
        g = g_ref[0, rows, :]
        for pr in range(GLA_HEADS // 2):
            sl = slice(pr * GLA_PAIR, (pr + 1) * GLA_PAIR)
            qp_p = qp[:, sl]
            q_heads = (jnp.where(lane_lo, qp_p, 0.0).astype(BF16), jnp.where(lane_lo, 0.0, qp_p).astype(BF16))
            attn = lax.dot_general(jnp.concatenate(q_heads, axis=0), kp[:, sl].astype(BF16),
                                   (((1,), (1,)), ((), ())), preferred_element_type=F32)
            s_old = s_ref[0, pr]
            s_old_b = s_old.astype(BF16)
            for j in range(2):
                hd = 2 * pr + j
                hs = slice(hd * GLA_DV, (hd + 1) * GLA_DV)
                a_j = jnp.where(tri, attn[j * c_sz:(j + 1) * c_sz], 0.0).astype(BF16)
                o = jnp.dot(jnp.concatenate([a_j, q_heads[j]], axis=1),
                            jnp.concatenate([v[:, hs], s_old_b], axis=0), preferred_element_type=F32)
                finish(rows, hd, o, g)
            kd_t = kd[:, sl].T
            kd_heads = jnp.concatenate([jnp.where(row_lo, kd_t, 0.0).astype(BF16),
                                        jnp.where(row_lo, 0.0, kd_t).astype(BF16)], axis=1)
            v_pair = jnp.concatenate([v[:, (2 * pr) * GLA_DV:(2 * pr + 1) * GLA_DV],
                                      v[:, (2 * pr + 1) * GLA_DV:(2 * pr + 2) * GLA_DV]], axis=0)
            s_ref[0, pr] = e_last_col[sl] * s_old + jnp.dot(kd_heads, v_pair, preferred_element_type=F32)
        return carry

    la_all = la_ref[0]
    total = jnp.concatenate([jnp.sum(la_all[c * c_sz:(c + 1) * c_sz], axis=0, keepdims=True)
                             for c in range(n_chunks)], axis=0)
    chunked_ok = jnp.min(total) >= -GLA_MAX_CHUNK_DECAY

    @pl.when(chunked_ok)
    def _():
        lax.fori_loop(0, n_chunks, chunk, 0, unroll=True)

    @pl.when(jnp.logical_not(chunked_ok))
    def _():
        lax.fori_loop(0, n_chunks, exact_chunk, 0)


def _gla_prompt(gq, gk, gla, gv, gg, g_norm, *, batch, tc):
    m = gq.shape[0]
    t = m // batch
    tc = min(tc, t)
    r3 = lambda a: a.reshape(batch, t, a.shape[1])
    blk = lambda width: pl.BlockSpec((1, tc, width), lambda b, i: (b, i, 0))
    kw, vw = GLA_HEADS * GLA_DK, GLA_HEADS * GLA_DV
    y, s = pl.pallas_call(
        _gla_prompt_kernel,
        grid=(batch, t // tc),
        in_specs=[blk(kw), blk(kw), blk(kw), blk(vw), blk(vw), _const_spec((1, GLA_DV))],
        out_specs=[blk(vw), pl.BlockSpec((1, GLA_HEADS // 2, GLA_PAIR, GLA_DV), lambda b, i: (b, 0, 0, 0))],
        out_shape=[jax.ShapeDtypeStruct((batch, t, vw), BF16),
                   jax.ShapeDtypeStruct((batch, GLA_HEADS // 2, GLA_PAIR, GLA_DV), F32)],
        scratch_shapes=[pltpu.VMEM((GLA_CHUNK, vw), F32)],
        compiler_params=_params("parallel", "arbitrary"),
        name="gla_prompt",
    )(r3(gq), r3(gk), r3(gla), r3(gv), r3(gg), g_norm.reshape(1, GLA_DV))
    return y.reshape(m, vw), s.reshape(batch, GLA_HEADS, GLA_DK, GLA_DV)


def _gla_sample_kernel(q_ref, k_ref, la_ref, v_ref, g_ref, gn_ref, s_ref, y_ref, so_ref):
    nb = q_ref.shape[0]
    q, k = q_ref[...], k_ref[...]
    a = jnp.exp(la_ref[...])
    qa = q * a
    v = v_ref[...].astype(F32)
    g = g_ref[...]
    lane_lo = lax.broadcasted_iota(jnp.int32, (nb, GLA_PAIR), 1) < GLA_DK
    qk = q * k
    qk_heads = (jnp.sum(jnp.where(lane_lo, qk, 0.0), axis=1, keepdims=True),
                jnp.sum(jnp.where(lane_lo, 0.0, qk), axis=1, keepdims=True))
    o = [qk_heads[j] * v[:, j * GLA_DV:(j + 1) * GLA_DV] for j in range(2)]
    for c in range(GLA_PAIR):
        j = c // GLA_DK
        s_c = s_ref[:, c, :]
        o[j] = o[j] + qa[:, c:c + 1] * s_c
        so_ref[:, c, :] = a[:, c:c + 1] * s_c + k[:, c:c + 1] * v[:, j * GLA_DV:(j + 1) * GLA_DV]
    gn = gn_ref[...]
    for j in range(2):
        hs = slice(j * GLA_DV, (j + 1) * GLA_DV)
        y_ref[:, hs] = (_rms(o[j], gn) * _silu(g[:, hs])).astype(BF16)


def _gla_sample(gq, gk, gla, gv, gg, g_norm, state):
    nb = gq.shape[0]
    kw, vw = GLA_HEADS * GLA_DK, GLA_HEADS * GLA_DV
    st = state.reshape(nb, kw, GLA_DV)
    lane_blk = lambda width: pl.BlockSpec((nb, width), lambda p: (0, p))
    st_blk = pl.BlockSpec((nb, GLA_PAIR, GLA_DV), lambda p: (0, p, 0))
    y, s = pl.pallas_call(
        _gla_sample_kernel,
        grid=(GLA_HEADS // 2,),
        in_specs=[lane_blk(GLA_PAIR), lane_blk(GLA_PAIR), lane_blk(GLA_PAIR), lane_blk(2 * GLA_DV),
                  lane_blk(2 * GLA_DV), _const_spec((1, GLA_DV)), st_blk],
        out_specs=[lane_blk(2 * GLA_DV), st_blk],
        out_shape=[jax.ShapeDtypeStruct((nb, vw), BF16), jax.ShapeDtypeStruct((nb, kw, GLA_DV), F32)],
        compiler_params=_params("parallel"),
        name="gla_sample",
    )(gq, gk, gla, gv, gg, g_norm.reshape(1, GLA_DV), st)
    return y, s.reshape(nb, GLA_HEADS, GLA_DK, GLA_DV)


def _mla_prompt_kernel(q_ref, k_ref, v_ref, o_ref, *, tq):
    t = q_ref.shape[1]
    contract_last = (((1,), (1,)), ((), ()))
    causal = (lax.broadcasted_iota(jnp.int32, (tq, tq), 1) <= lax.broadcasted_iota(jnp.int32, (tq, tq), 0))
    for qi in range(t // tq):
        q0, kend = qi * tq, (qi + 1) * tq
        q = q_ref[0, q0:kend, :]
        s_d = lax.dot_general(q, k_ref[0, q0:kend, :], contract_last, preferred_element_type=F32)
        s_d = jnp.where(causal, s_d, -jnp.inf)
        mx = jnp.max(s_d, axis=1, keepdims=True)
        if qi:
            s_o = lax.dot_general(q, k_ref[0, 0:q0, :], contract_last, preferred_element_type=F32)
            mx = jnp.maximum(mx, jnp.max(s_o, axis=1, keepdims=True))
        p_d = jnp.exp2(s_d - mx)
        denom = jnp.sum(p_d, axis=1, keepdims=True)
        o = jnp.dot(p_d.astype(BF16), v_ref[0, q0:kend, :], preferred_element_type=F32)
        if qi:
            p_o = jnp.exp2(s_o - mx)
            denom = denom + jnp.sum(p_o, axis=1, keepdims=True)
            o = o + jnp.dot(p_o.astype(BF16), v_ref[0, 0:q0, :], preferred_element_type=F32)
        o_ref[0, q0:kend, :] = (o / denom).astype(BF16)


def _mla_prompt(mq, mk, mv, *, batch, tq):
    m = mq.shape[0]
    t = m // batch
    tq = min(tq, t)
    qk_blk = pl.BlockSpec((1, t, _QK_HEAD), lambda b, h: (b, 0, h))
    v_blk = pl.BlockSpec((1, t, MLA_V), lambda b, h: (b, 0, h))
    y = pl.pallas_call(
        functools.partial(_mla_prompt_kernel, tq=tq),
        grid=(batch, MLA_HEADS),
        in_specs=[qk_blk, qk_blk, v_blk],
        out_specs=v_blk,
        out_shape=jax.ShapeDtypeStruct((batch, t, MLA_HEADS * MLA_V), BF16),
        compiler_params=_params("parallel", "parallel"),
        name="mla_prompt",
    )(mq.reshape(batch, t, -1), mk.reshape(batch, t, -1), mv.reshape(batch, t, -1))
    return y.reshape(m, MLA_HEADS * MLA_V)


def _mla_qprep_kernel(mq_ref, wuk_ref, qlat_ref, qpe_ref):
    for hd in range(MLA_HEADS):
        q0 = hd * _QK_HEAD
        hs = slice(hd * LANES, (hd + 1) * LANES)
        qlat_ref[:, hs] = jnp.dot(mq_ref[:, q0:q0 + LANES], wuk_ref[hd], preferred_element_type=F32)
        qpe_ref[:, hs] = mq_ref[:, q0 + LANES:q0 + 2 * LANES].astype(F32)


def _mla_oproj_kernel(ol_ref, wuv_ref, y_ref):
    for hd in range(MLA_HEADS):
        hs = slice(hd * LANES, (hd + 1) * LANES)
        y_ref[:, hs] = jnp.dot(ol_ref[:, hs].astype(BF16), wuv_ref[hd],
                               preferred_element_type=F32).astype(BF16)


def _mla_decode_kernel(pt_ref, ql_ref, qp_ref, latn_ref, pen_ref, ckv_ref, cpe_ref,
                       o_ref, lat_buf, pe_buf, sems, *, page, n_pages):
    b = pl.program_id(0)
    nb = pl.num_programs(0)
    slot = b % 2

    def page_copies(bb, sl, p):
        pg = pt_ref[bb, p]
        rows = pl.ds(pl.multiple_of(p * page, page), page)
        return (pltpu.make_async_copy(ckv_ref.at[pg], lat_buf.at[sl, rows, :], sems.at[0, sl]),
                pltpu.make_async_copy(cpe_ref.at[pg], pe_buf.at[sl, :, rows], sems.at[1, sl]))

    def start_pages(bb, sl):
        def body(p, carry):
            for cp in page_copies(bb, sl, p):
                cp.start()
            return carry
        lax.fori_loop(0, n_pages, body, 0, unroll=8)

    def wait_pages(sl):
        pltpu.make_async_copy(lat_buf.at[sl], lat_buf.at[sl], sems.at[0, sl]).wait()
        pltpu.make_async_copy(pe_buf.at[sl], pe_buf.at[sl], sems.at[1, sl]).wait()

    @pl.when(b == 0)
    def _():
        start_pages(0, 0)

    @pl.when(b + 1 < nb)
    def _():
        start_pages(b + 1, 1 - slot)

    wait_pages(slot)

    pad = jnp.zeros((_Q_ROWS - MLA_HEADS, LANES), F32)
    ql = jnp.concatenate([ql_ref[0], pad], axis=0)
    qp = jnp.concatenate([qp_ref[0], pad], axis=0)[:, :MLA_ROPE]
    lat_b = lat_buf[slot]
    pe_b = pe_buf[slot]
    s = (lax.dot_general(ql, lat_b, (((1,), (1,)), ((), ())), preferred_element_type=F32)
         + jnp.dot(qp, pe_b, preferred_element_type=F32))
    lat_new = latn_ref[0]
    s_new = (jnp.sum(ql * lat_new, axis=1, keepdims=True)
             + jnp.sum(qp * pen_ref[0], axis=1, keepdims=True))
    mx = jnp.maximum(jnp.max(s, axis=1, keepdims=True), s_new)
    p = jnp.exp2(s - mx)
    p_new = jnp.exp2(s_new - mx)
    denom = jnp.sum(p, axis=1, keepdims=True) + p_new
    o = (jnp.dot(p, lat_b, preferred_element_type=F32) + p_new * lat_new) / denom
    o_ref[0] = o[:MLA_HEADS]


def _mla_decode(mq, lat_new, pe_new, cache_kv, cache_pe, page_table, w_uk_t, w_uv_h):
    nb, n_pages = page_table.shape
    page = cache_kv.shape[1]
    past = n_pages * page
    hw = MLA_HEADS * LANES
    qlat, qpe = pl.pallas_call(
        _mla_qprep_kernel,
        out_shape=[jax.ShapeDtypeStruct((nb, hw), F32), jax.ShapeDtypeStruct((nb, hw), F32)],
        name="mla_qprep",
    )(mq, w_uk_t)
    per_b = lambda rows, width: pl.BlockSpec((1, rows, width), lambda b, pt: (b, 0, 0))
    grid_spec = pltpu.PrefetchScalarGridSpec(
        num_scalar_prefetch=1,
        grid=(nb,),
        in_specs=[per_b(MLA_HEADS, LANES), per_b(MLA_HEADS, LANES), per_b(1, MLA_KV_RANK), per_b(1, MLA_ROPE),
                  pl.BlockSpec(memory_space=pl.ANY), pl.BlockSpec(memory_space=pl.ANY)],
        out_specs=per_b(MLA_HEADS, MLA_KV_RANK),
        scratch_shapes=[pltpu.VMEM((2, past, MLA_KV_RANK), F32), pltpu.VMEM((2, MLA_ROPE, past), F32),
                        pltpu.SemaphoreType.DMA((2, 2))],
    )
    o_lat = pl.pallas_call(
        functools.partial(_mla_decode_kernel, page=page, n_pages=n_pages),
        grid_spec=grid_spec,
        out_shape=jax.ShapeDtypeStruct((nb, MLA_HEADS, MLA_KV_RANK), F32),
        compiler_params=_params("arbitrary"),
        name="mla_decode",
    )(page_table, qlat.reshape(nb, MLA_HEADS, LANES), qpe.reshape(nb, MLA_HEADS, LANES),
      lat_new.reshape(nb, 1, MLA_KV_RANK), pe_new.reshape(nb, 1, MLA_ROPE), cache_kv, cache_pe)
    return pl.pallas_call(
        _mla_oproj_kernel,
        out_shape=jax.ShapeDtypeStruct((nb, MLA_HEADS * MLA_V), BF16),
        name="mla_oproj",
    )(o_lat.reshape(nb, hw), w_uv_h)


def _rope_tables(pos):
    half = MLA_ROPE // 2
    inv_freq = jnp.power(jnp.float32(ROPE_THETA), -jnp.arange(half, dtype=F32) / half)
    ang = pos.astype(F32)[:, None] * inv_freq[None, :]
    cos, sin = jnp.cos(ang), jnp.sin(ang)
    z = jnp.zeros((pos.shape[0], LANES - MLA_ROPE), F32)
    return jnp.concatenate([cos, cos, z], axis=1), jnp.concatenate([sin, sin, z], axis=1)


def _rot_cols(w):
    half = MLA_ROPE // 2
    return jnp.concatenate([-w[:, half:], w[:, :half]], axis=1)


def _prep_mixer_weights(w_in, w_a_up, b_a, q_norm, w_uq, kv_norm, w_uk, w_uv):
    d = w_in.shape[0]
    gw, vw = GLA_HEADS * GLA_DK, GLA_HEADS * GLA_DV
    offs, o = [], 0
    for n in (gw, gw, vw, vw, GLA_GATE_RANK, MLA_Q_RANK, MLA_KV_RANK, MLA_ROPE):
        offs.append((o, o + n))
        o += n
    q, k, v, g, a_low, c_q, c_kv, k_pe = (w_in[:, a:b] for a, b in offs)
    zeros = lambda rows, n: jnp.zeros((rows, n), F32)
    z_half = zeros(d, LANES - MLA_ROPE)
    w_in_x = jnp.concatenate([q, k, v, g, c_q, c_kv, k_pe, z_half, _rot_cols(k_pe), z_half,
                              a_low, zeros(d, LANES - GLA_GATE_RANK)], axis=1).astype(BF16)
    w_a = jnp.concatenate([w_a_up, zeros(LANES - GLA_GATE_RANK, gw)], axis=0).astype(BF16)
    zq = zeros(MLA_Q_RANK, LANES - MLA_ROPE)
    cols = []
    for hd in range(MLA_HEADS):
        c0 = hd * (MLA_NOPE + MLA_ROPE)
        pe = w_uq[:, c0 + MLA_NOPE:c0 + MLA_NOPE + MLA_ROPE]
        cols += [w_uq[:, c0:c0 + MLA_NOPE], pe, zq, _rot_cols(pe), zq]
    w_uq_x = jnp.concatenate(cols, axis=1).astype(BF16)
    return {"w_in": w_in_x, "w_a": w_a, "b_a": b_a.reshape(1, gw), "q_norm": q_norm.reshape(1, -1),
            "w_uq": w_uq_x, "kv_norm": kv_norm.reshape(1, -1),
            "w_ukv": jnp.concatenate([w_uk.reshape(MLA_KV_RANK, -1), w_uv.reshape(MLA_KV_RANK, -1)],
                                     axis=1).astype(BF16)}


def kernel(x_prompt, x_sample, cache_kv, cache_pe, state_gla, page_table, ffn1_norm_w, ffn1_w_gate, ffn1_w_up, ffn1_w_down, mix_norm_w, w_in, gla_w_a_up, gla_b_a, gla_norm_w, mla_q_norm_w, mla_w_uq, mla_kv_norm_w, mla_w_uk, mla_w_uv, w_out, ffn2_norm_w, ffn2_w_gate, ffn2_w_up, ffn2_w_down, final_norm_w):
    assert cache_kv.shape[0] == 1, "single-layer trunk"
    batch, seq, d = x_prompt.shape
    nb, dec_seq, _ = x_sample.shape
    assert dec_seq == 1
    page = cache_kv.shape[2]
    past = page_table.shape[1] * page

    f1 = (ffn1_w_gate[0].astype(BF16), ffn1_w_up[0].astype(BF16), ffn1_w_down[0].astype(BF16))
    f2 = (ffn2_w_gate[0].astype(BF16), ffn2_w_up[0].astype(BF16), ffn2_w_down[0].astype(BF16))
    wo = w_out[0].astype(BF16)
    mw = _prep_mixer_weights(w_in[0], gla_w_a_up[0], gla_b_a[0], mla_q_norm_w[0], mla_w_uq[0],
                             mla_kv_norm_w[0], mla_w_uk[0], mla_w_uv[0])
    w_uk_t = jnp.transpose(mla_w_uk[0], (1, 2, 0)).astype(BF16)
    w_uv_h = jnp.transpose(mla_w_uv[0], (1, 0, 2)).astype(BF16)

    tm, tm_ffn = 256, 512
    xp = x_prompt.reshape(batch * seq, d)
    xp = _ffn(xp, ffn1_norm_w[0], *f1, tm=tm_ffn)
    cos, sin = _rope_tables(jnp.arange(seq, dtype=jnp.int32))
    gq, gk, gla, gv, gg, mq, mk, mv, lat, kpe = _inproj(xp, mix_norm_w[0], mw, cos, sin, tm=tm)
    y_gla, s_prompt = _gla_prompt(gq, gk, gla, gv, gg, gla_norm_w[0], batch=batch, tc=512)
    y_mla = _mla_prompt(mq, mk, mv, batch=batch, tq=256)
    yp = _ffn(xp, ffn2_norm_w[0], *f2, mix=(y_gla, y_mla, wo), final_w=final_norm_w, tm=tm_ffn)

    xs = x_sample.reshape(nb, d)
    xs = _ffn(xs, ffn1_norm_w[0], *f1, tm=nb)
    cos_s, sin_s = _rope_tables(jnp.full((nb,), past, dtype=jnp.int32))
    sq, sk, sla, sv, sg, smq, _, _, slat, skpe = _inproj(xs, mix_norm_w[0], mw, cos_s, sin_s, tm=nb)
    ys_gla, s_sample = _gla_sample(sq, sk, sla, sv, sg, gla_norm_w[0], state_gla[0])
    ys_mla = _mla_decode(smq, slat, skpe, cache_kv[0], jnp.swapaxes(cache_pe[0], 1, 2), page_table,
                         w_uk_t, w_uv_h)
    ys = _ffn(xs, ffn2_norm_w[0], *f2, mix=(ys_gla, ys_mla, wo), final_w=final_norm_w, tm=nb)

    return (yp.reshape(batch, seq, d), ys.reshape(nb, 1, d),
            lat.reshape(1, batch, seq, MLA_KV_RANK), kpe.reshape(1, batch, seq, MLA_ROPE),
            s_prompt[None], slat.reshape(1, nb, 1, MLA_KV_RANK), skpe.reshape(1, nb, 1, MLA_ROPE),
            s_sample[None])
```

```python
import functools
import math

import jax
import jax.numpy as jnp
from jax import lax
from jax.experimental import pallas as pl
from jax.experimental.pallas import tpu as pltpu

F32 = jnp.float32
BF16 = jnp.bfloat16

EPS = 1e-6
GLA_HEADS = 4
GLA_DK = 64
GLA_DV = 128
GLA_GATE_RANK = 16
GLA_GATE_NORM = 16.0
MLA_HEADS = 4
MLA_NOPE = 128
MLA_ROPE = 64
MLA_V = 128
MLA_Q_RANK = 256
MLA_KV_RANK = 128
ROPE_THETA = 10000.0
MLA_SCALE = 1.0 / math.sqrt(MLA_NOPE + MLA_ROPE)
MLA_Q_SCALE = MLA_SCALE * math.log2(math.e)

LANES = 128
GLA_CHUNK = 128
GLA_PAIR = 2 * GLA_DK
GLA_MAX_CHUNK_DECAY = 60.0
VMEM_LIMIT = 56 * 1024 * 1024


def _rms(x, w):
    return x * lax.rsqrt(jnp.mean(x * x, axis=-1, keepdims=True) + EPS) * w


def _silu(x):
    return x * jax.nn.sigmoid(x)


def _const_spec(shape):
    nd = len(shape)
    return pl.BlockSpec(shape, lambda *_: (0,) * nd, pipeline_mode=pl.Buffered(1))


def _params(*sem):
    return pltpu.CompilerParams(dimension_semantics=sem, vmem_limit_bytes=VMEM_LIMIT)


def _ffn_kernel(*refs, has_mix, has_final):
    refs = list(refs)
    x_ref = refs.pop(0)
    if has_mix:
        ya_ref, yb_ref, wo_ref = refs.pop(0), refs.pop(0), refs.pop(0)
    nw_ref, wg_ref, wu_ref, wd_ref = refs.pop(0), refs.pop(0), refs.pop(0), refs.pop(0)
    if has_final:
        fw_ref = refs.pop(0)
    (o_ref,) = refs

    x = x_ref[...]
    if has_mix:
        y = jnp.concatenate([ya_ref[...], yb_ref[...]], axis=-1)
        x = x + jnp.dot(y, wo_ref[...], preferred_element_type=F32)
    h = _rms(x, nw_ref[...]).astype(BF16)
    g = jnp.dot(h, wg_ref[...], preferred_element_type=F32)
    u = jnp.dot(h, wu_ref[...], preferred_element_type=F32)
    a = (_silu(g) * u).astype(BF16)
    x = x + 0.5 * jnp.dot(a, wd_ref[...], preferred_element_type=F32)
    if has_final:
        x = _rms(x, fw_ref[...])
    o_ref[...] = x


def _ffn(x, norm_w, wg, wu, wd, *, mix=None, final_w=None, tm):
    m, d = x.shape
    ff = wg.shape[1]
    tm = min(tm, m)
    row = lambda i: (i, 0)
    args, specs = [x], [pl.BlockSpec((tm, d), row)]
    if mix is not None:
        ya, yb, wo = mix
        args += [ya, yb, wo]
        specs += [pl.BlockSpec((tm, ya.shape[1]), row), pl.BlockSpec((tm, yb.shape[1]), row),
                  _const_spec(wo.shape)]
    args += [norm_w.reshape(1, d), wg, wu, wd]
    specs += [_const_spec((1, d)), _const_spec((d, ff)), _const_spec((d, ff)), _const_spec((ff, d))]
    if final_w is not None:
        args.append(final_w.reshape(1, d))
        specs.append(_const_spec((1, d)))
    return pl.pallas_call(
        functools.partial(_ffn_kernel, has_mix=mix is not None, has_final=final_w is not None),
        grid=(m // tm,),
        in_specs=specs,
        out_specs=pl.BlockSpec((tm, d), row),
        out_shape=jax.ShapeDtypeStruct((m, d), F32),
        compiler_params=_params("parallel"),
        name="ffn_mix_final" if mix is not None else "ffn",
    )(*args)


_IN_GQ, _IN_GK, _IN_GV, _IN_GG, _IN_CQ, _IN_CKV, _IN_KPE, _IN_KROT, _IN_ALOW, _IN_END = (
    0, 256, 512, 1024, 1536, 1792, 1920, 2048, 2176, 2304)
_UQ_HEAD = 3 * LANES
_QK_HEAD = 2 * LANES
_Q_ROWS = 16


def _inproj_kernel(x_ref, nw_ref, win_ref, wa_ref, ba_ref, qn_ref, wuq_ref, kvn_ref, wukv_ref,
                   cos_ref, sin_ref,
                   gq_ref, gk_ref, gla_ref, gv_ref, gg_ref, mq_ref, mk_ref, mv_ref, lat_ref, kpe_ref):
    h = _rms(x_ref[...], nw_ref[...]).astype(BF16)
    p = jnp.dot(h, win_ref[...], preferred_element_type=F32)
    cos, sin = cos_ref[...], sin_ref[...]

    gq_ref[...] = p[:, _IN_GQ:_IN_GK] * (GLA_DK ** -0.5)
    gk_ref[...] = p[:, _IN_GK:_IN_GV]
    gv_ref[...] = p[:, _IN_GV:_IN_GG].astype(BF16)
    gg_ref[...] = p[:, _IN_GG:_IN_CQ]
    z = jnp.dot(p[:, _IN_ALOW:_IN_END].astype(BF16), wa_ref[...], preferred_element_type=F32) + ba_ref[...]
    log_sig = jnp.minimum(z, 0.0) - jnp.log1p(jnp.exp(-jnp.abs(z)))
    gla_ref[...] = log_sig * (1.0 / GLA_GATE_NORM)

    cq = _rms(p[:, _IN_CQ:_IN_CKV], qn_ref[...]).astype(BF16)
    qx = jnp.dot(cq, wuq_ref[...], preferred_element_type=F32)
    lat = _rms(p[:, _IN_CKV:_IN_KPE], kvn_ref[...])
    lat_ref[...] = lat
    kpe = p[:, _IN_KPE:_IN_KROT] * cos + p[:, _IN_KROT:_IN_ALOW] * sin
    kpe_ref[...] = kpe[:, :MLA_ROPE]
    kv = jnp.dot(lat.astype(BF16), wukv_ref[...], preferred_element_type=F32)
    kpe_b = kpe.astype(BF16)
    for hd in range(MLA_HEADS):
        q0 = hd * _UQ_HEAD
        q_pe = qx[:, q0 + LANES:q0 + 2 * LANES] * cos + qx[:, q0 + 2 * LANES:q0 + 3 * LANES] * sin
        o0 = hd * _QK_HEAD
        mq_ref[:, o0:o0 + LANES] = (qx[:, q0:q0 + LANES] * MLA_Q_SCALE).astype(BF16)
        mq_ref[:, o0 + LANES:o0 + 2 * LANES] = (q_pe * MLA_Q_SCALE).astype(BF16)
        mk_ref[:, o0:o0 + LANES] = kv[:, hd * MLA_NOPE:(hd + 1) * MLA_NOPE].astype(BF16)
        mk_ref[:, o0 + LANES:o0 + 2 * LANES] = kpe_b
    mv_ref[...] = kv[:, MLA_HEADS * MLA_NOPE:].astype(BF16)


def _inproj(x, norm_w, w, cos, sin, *, tm):
    m, d = x.shape
    tm = min(tm, m)
    n_tab = cos.shape[0] // tm
    row = lambda i: (i, 0)
    tab = lambda i: (i % n_tab, 0)
    gw, mw = GLA_HEADS * GLA_DK, MLA_HEADS * _QK_HEAD
    outs = [((m, gw), F32), ((m, gw), F32), ((m, gw), F32), ((m, GLA_HEADS * GLA_DV), BF16),
            ((m, GLA_HEADS * GLA_DV), F32), ((m, mw), BF16), ((m, mw), BF16),
            ((m, MLA_HEADS * MLA_V), BF16), ((m, MLA_KV_RANK), F32), ((m, MLA_ROPE), F32)]
    consts = [norm_w.reshape(1, d), w["w_in"], w["w_a"], w["b_a"], w["q_norm"], w["w_uq"],
              w["kv_norm"], w["w_ukv"]]
    out_specs = [pl.BlockSpec((tm, s[1]), row) for s, _ in outs]
    return pl.pallas_call(
        _inproj_kernel,
        grid=(m // tm,),
        in_specs=[pl.BlockSpec((tm, d), row)] + [_const_spec(c.shape) for c in consts]
        + [pl.BlockSpec((tm, LANES), tab), pl.BlockSpec((tm, LANES), tab)],
        out_specs=out_specs,
        out_shape=[jax.ShapeDtypeStruct(s, dt) for s, dt in outs],
        compiler_params=_params("parallel"),
        name="inproj",
    )(x, *consts, cos, sin)


def _split3(x):
    hi = x.astype(BF16)
    r = x - hi.astype(F32)
    mid = r.astype(BF16)
    lo = (r - mid.astype(F32)).astype(BF16)
    return hi, mid, lo


def _gla_prompt_kernel(q_ref, k_ref, la_ref, v_ref, g_ref, gn_ref, y_ref, s_ref, o_scr):
    c_sz = GLA_CHUNK
    n_chunks = q_ref.shape[1] // c_sz

    @pl.when(pl.program_id(1) == 0)
    def _():
        s_ref[...] = jnp.zeros_like(s_ref)

    row = lax.broadcasted_iota(jnp.int32, (c_sz, c_sz), 0)
    col = lax.broadcasted_iota(jnp.int32, (c_sz, c_sz), 1)
    tri = row >= col
    tri_b = jnp.where(tri, 1.0, 0.0).astype(BF16)
    lane_lo = col < GLA_DK
    row_lo = row < GLA_DK
    gn = gn_ref[...]

    def finish(rows, hd, o, g):
        hs = slice(hd * GLA_DV, (hd + 1) * GLA_DV)
        y_ref[0, rows, hs] = (_rms(o, gn) * _silu(g[:, hs])).astype(BF16)

    def exact_chunk(c, carry):
        rows = pl.ds(pl.multiple_of(c * c_sz, c_sz), c_sz)
        a_t = jnp.exp(la_ref[0, rows, :]).T
        k_t = k_ref[0, rows, :].T
        q_t = q_ref[0, rows, :].T
        v = v_ref[0, rows, :].astype(F32)
        tok_lane = lax.broadcasted_iota(jnp.int32, a_t.shape, 1)
        tok_row = lax.broadcasted_iota(jnp.int32, v.shape, 0)
        o_scr[...] = jnp.zeros_like(o_scr)

        def token(t, carry_t):
            pick = tok_lane == t
            a_c = jnp.sum(jnp.where(pick, a_t, 0.0), axis=1, keepdims=True)
            k_c = jnp.sum(jnp.where(pick, k_t, 0.0), axis=1, keepdims=True)
            q_c = jnp.sum(jnp.where(pick, q_t, 0.0), axis=1, keepdims=True)
            v_r = jnp.sum(jnp.where(tok_row == t, v, 0.0), axis=0, keepdims=True)
            o_heads = []
            for pr in range(GLA_HEADS // 2):
                sl = slice(pr * GLA_PAIR, (pr + 1) * GLA_PAIR)
                v_lo = v_r[:, (2 * pr) * GLA_DV:(2 * pr + 1) * GLA_DV]
                v_hi = v_r[:, (2 * pr + 1) * GLA_DV:(2 * pr + 2) * GLA_DV]
                s_new = a_c[sl] * s_ref[0, pr] + k_c[sl] * jnp.where(row_lo, v_lo, v_hi)
                s_ref[0, pr] = s_new
                qs = q_c[sl] * s_new
                o_heads.append(jnp.sum(jnp.where(row_lo, qs, 0.0), axis=0, keepdims=True))
                o_heads.append(jnp.sum(jnp.where(row_lo, 0.0, qs), axis=0, keepdims=True))
            o_scr[...] = jnp.where(tok_row == t, jnp.concatenate(o_heads, axis=1), o_scr[...])
            return carry_t

        lax.fori_loop(0, c_sz, token, 0)
        g = g_ref[0, rows, :]
        for hd in range(GLA_HEADS):
            finish(rows, hd, o_scr[:, hd * GLA_DV:(hd + 1) * GLA_DV], g)
        return carry

    def chunk(c, carry):
        r0 = pl.multiple_of(c * c_sz, c_sz)
        rows = pl.ds(r0, c_sz)
        la = la_ref[0, rows, :]
        b = sum(jnp.dot(tri_b, t, preferred_element_type=F32) for t in _split3(la))
        e_pos = jnp.exp(b)
        qp = q_ref[0, rows, :] * e_pos
        kp = k_ref[0, rows, :] * jnp.exp(-b)
        kd = kp * e_pos[c_sz - 1:c_sz, :]
        e_last_col = jnp.exp(jnp.sum(la.T, axis=1, keepdims=True))
        v = v_ref[0, rows, :]
        g = g_ref[0, rows, :]
        for pr in range(GLA_HEADS // 2):
            sl = slice(pr * GLA_PAIR, (pr + 1) * GLA_PAIR)
            qp_p = qp[:, sl]
            q_heads = (jnp.where(lane_lo, qp_p, 0.0).astype(BF16), jnp.where(lane_lo, 0.0, qp_p).astype(BF16))
            attn = lax.dot_general(jnp.concatenate(q_heads, axis=0), kp[:, sl].astype(BF16),
                                   (((1,), (1,)), ((), ())), preferred_element_type=F32)
            s_old = s_ref[0, pr]
            s_old_b = s_old.astype(BF16)
            for j in range(2):
                hd = 2 * pr + j
                hs = slice(hd * GLA_DV, (hd + 1) * GLA_DV)
                a_j = jnp.where(tri, attn[j * c_sz:(j + 1) * c_sz], 0.0).astype(BF16)
                o = jnp.dot(jnp.concatenate([a_j, q_heads[j]], axis=1),
                            jnp.concatenate([v[:, hs], s_old_b], axis=0), preferred_element_type=F32)
                finish(rows, hd, o, g)
            kd_t = kd[:, sl].T
            kd_heads = jnp.concatenate([jnp.where(row_lo, kd_t, 0.0).astype(BF16),
                                        jnp.where(row_lo, 0.0, kd_t).astype(BF16)], axis=1)
            v_pair = jnp.concatenate([v[:, (2 * pr) * GLA_DV:(2 * pr + 1) * GLA_DV],
                                      v[:, (2 * pr + 1) * GLA_DV:(2 * pr + 2) * GLA_DV]], axis=0)
            s_ref[0, pr] = e_last_col[sl] * s_old + jnp.dot(kd_heads, v_pair, preferred_element_type=F32)
        return carry

    la_all = la_ref[0]
    total = jnp.concatenate([jnp.sum(la_all[c * c_sz:(c + 1) * c_sz], axis=0, keepdims=True)
                             for c in range(n_chunks)], axis=0)
    chunked_ok = jnp.min(total) >= -GLA_MAX_CHUNK_DECAY

    @pl.when(chunked_ok)
    def _():
        lax.fori_loop(0, n_chunks, chunk, 0, unroll=True)

    @pl.when(jnp.logical_not(chunked_ok))
    def _():
        lax.fori_loop(0, n_chunks, exact_chunk, 0)


def _gla_prompt(gq, gk, gla, gv, gg, g_norm, *, batch, tc):
    m = gq.shape[0]
    t = m // batch
    tc = min(tc, t)
    r3 = lambda a: a.reshape(batch, t, a.shape[1])
    blk = lambda width: pl.BlockSpec((1, tc, width), lambda b, i: (b, i, 0))
    kw, vw = GLA_HEADS * GLA_DK, GLA_HEADS * GLA_DV
    y, s = pl.pallas_call(
        _gla_prompt_kernel,
        grid=(batch, t // tc),
        in_specs=[blk(kw), blk(kw), blk(kw), blk(vw), blk(vw), _const_spec((1, GLA_DV))],
        out_specs=[blk(vw), pl.BlockSpec((1, GLA_HEADS // 2, GLA_PAIR, GLA_DV), lambda b, i: (b, 0, 0, 0))],
        out_shape=[jax.ShapeDtypeStruct((batch, t, vw), BF16),
                   jax.ShapeDtypeStruct((batch, GLA_HEADS // 2, GLA_PAIR, GLA_DV), F32)],
        scratch_shapes=[pltpu.VMEM((GLA_CHUNK, vw), F32)],
        compiler_params=_params("parallel", "arbitrary"),
        name="gla_prompt",
    )(r3(gq), r3(gk), r3(gla), r3(gv), r3(gg), g_norm.reshape(1, GLA_DV))
    return y.reshape(m, vw), s.reshape(batch, GLA_HEADS, GLA_DK, GLA_DV)


def _gla_sample_kernel(q_ref, k_ref, la_ref, v_ref, g_ref, gn_ref, s_ref, y_ref, so_ref):
    nb = q_ref.shape[0]
    q, k = q_ref[...], k_ref[...]
    a = jnp.exp(la_ref[...])
    qa = q * a
    v = v_ref[...].astype(F32)
    g = g_ref[...]
    lane_lo = lax.broadcasted_iota(jnp.int32, (nb, GLA_PAIR), 1) < GLA_DK
    qk = q * k
    qk_heads = (jnp.sum(jnp.where(lane_lo, qk, 0.0), axis=1, keepdims=True),
                jnp.sum(jnp.where(lane_lo, 0.0, qk), axis=1, keepdims=True))
    o = [qk_heads[j] * v[:, j * GLA_DV:(j + 1) * GLA_DV] for j in range(2)]
    for c in range(GLA_PAIR):
        j = c // GLA_DK
        s_c = s_ref[:, c, :]
        o[j] = o[j] + qa[:, c:c + 1] * s_c
        so_ref[:, c, :] = a[:, c:c + 1] * s_c + k[:, c:c + 1] * v[:, j * GLA_DV:(j + 1) * GLA_DV]
    gn = gn_ref[...]
    for j in range(2):
        hs = slice(j * GLA_DV, (j + 1) * GLA_DV)
        y_ref[:, hs] = (_rms(o[j], gn) * _silu(g[:, hs])).astype(BF16)


def _gla_sample(gq, gk, gla, gv, gg, g_norm, state):
    nb = gq.shape[0]
    kw, vw = GLA_HEADS * GLA_DK, GLA_HEADS * GLA_DV
    st = state.reshape(nb, kw, GLA_DV)
    lane_blk = lambda width: pl.BlockSpec((nb, width), lambda p: (0, p))
    st_blk = pl.BlockSpec((nb, GLA_PAIR, GLA_DV), lambda p: (0, p, 0))
    y, s = pl.pallas_call(
        _gla_sample_kernel,
        grid=(GLA_HEADS // 2,),
        in_specs=[lane_blk(GLA_PAIR), lane_blk(GLA_PAIR), lane_blk(GLA_PAIR), lane_blk(2 * GLA_DV),
                  lane_blk(2 * GLA_DV), _const_spec((1, GLA_DV)), st_blk],
        out_specs=[lane_blk(2 * GLA_DV), st_blk],
        out_shape=[jax.ShapeDtypeStruct((nb, vw), BF16), jax.ShapeDtypeStruct((nb, kw, GLA_DV), F32)],
        compiler_params=_params("parallel"),
        name="gla_sample",
    )(gq, gk, gla, gv, gg, g_norm.reshape(1, GLA_DV), st)
    return y, s.reshape(nb, GLA_HEADS, GLA_DK, GLA_DV)


def _mla_prompt_kernel(q_ref, k_ref, v_ref, o_ref, *, tq):
    t = q_ref.shape[1]
    contract_last = (((1,), (1,)), ((), ()))
    causal = (lax.broadcasted_iota(jnp.int32, (tq, tq), 1) <= lax.broadcasted_iota(jnp.int32, (tq, tq), 0))
    for qi in range(t // tq):
        q0, kend = qi * tq, (qi + 1) * tq
        q = q_ref[0, q0:kend, :]
        s_d = lax.dot_general(q, k_ref[0, q0:kend, :], contract_last, preferred_element_type=F32)
        s_d = jnp.where(causal, s_d, -jnp.inf)
        mx = jnp.max(s_d, axis=1, keepdims=True)
        if qi:
            s_o = lax.dot_general(q, k_ref[0, 0:q0, :], contract_last, preferred_element_type=F32)
            mx = jnp.maximum(mx, jnp.max(s_o, axis=1, keepdims=True))
        p_d = jnp.exp2(s_d - mx)
        denom = jnp.sum(p_d, axis=1, keepdims=True)
        o = jnp.dot(p_d.astype(BF16), v_ref[0, q0:kend, :], preferred_element_type=F32)
        if qi:
            p_o = jnp.exp2(s_o - mx)
            denom = denom + jnp.sum(p_o, axis=1, keepdims=True)
            o = o + jnp.dot(p_o.astype(BF16), v_ref[0, 0:q0, :], preferred_element_type=F32)
        o_ref[0, q0:kend, :] = (o / denom).astype(BF16)


def _mla_prompt(mq, mk, mv, *, batch, tq):
    m = mq.shape[0]
    t = m // batch
    tq = min(tq, t)
    qk_blk = pl.BlockSpec((1, t, _QK_HEAD), lambda b, h: (b, 0, h))
    v_blk = pl.BlockSpec((1, t, MLA_V), lambda b, h: (b, 0, h))
    y = pl.pallas_call(
        functools.partial(_mla_prompt_kernel, tq=tq),
        grid=(batch, MLA_HEADS),
        in_specs=[qk_blk, qk_blk, v_blk],
        out_specs=v_blk,
        out_shape=jax.ShapeDtypeStruct((batch, t, MLA_HEADS * MLA_V), BF16),
        compiler_params=_params("parallel", "parallel"),
        name="mla_prompt",
    )(mq.reshape(batch, t, -1), mk.reshape(batch, t, -1), mv.reshape(batch, t, -1))
    return y.reshape(m, MLA_HEADS * MLA_V)


def _mla_qprep_kernel(mq_ref, wuk_ref, qlat_ref, qpe_ref):
    for hd in range(MLA_HEADS):
        q0 = hd * _QK_HEAD
        hs = slice(hd * LANES, (hd + 1) * LANES)
        qlat_ref[:, hs] = jnp.dot(mq_ref[:, q0:q0 + LANES], wuk_ref[hd], preferred_element_type=F32)
        qpe_ref[:, hs] = mq_ref[:, q0 + LANES:q0 + 2 * LANES].astype(F32)


def _mla_oproj_kernel(ol_ref, wuv_ref, y_ref):
    for hd in range(MLA_HEADS):
        hs = slice(hd * LANES, (hd + 1) * LANES)
        y_ref[:, hs] = jnp.dot(ol_ref[:, hs].astype(BF16), wuv_ref[hd],
                               preferred_element_type=F32).astype(BF16)


def _mla_decode_kernel(pt_ref, ql_ref, qp_ref, latn_ref, pen_ref, ckv_ref, cpe_ref,
                       o_ref, lat_buf, pe_buf, sems, *, page, n_pages):
    b = pl.program_id(0)
    nb = pl.num_programs(0)
    slot = b % 2

    def page_copies(bb, sl, p):
        pg = pt_ref[bb, p]
        rows = pl.ds(pl.multiple_of(p * page, page), page)
        return (pltpu.make_async_copy(ckv_ref.at[pg], lat_buf.at[sl, rows, :], sems.at[0, sl]),
                pltpu.make_async_copy(cpe_ref.at[pg], pe_buf.at[sl, :, rows], sems.at[1, sl]))

    def start_pages(bb, sl):
        def body(p, carry):
            for cp in page_copies(bb, sl, p):
                cp.start()
            return carry
        lax.fori_loop(0, n_pages, body, 0, unroll=8)

    def wait_pages(sl):
        pltpu.make_async_copy(lat_buf.at[sl], lat_buf.at[sl], sems.at[0, sl]).wait()
        pltpu.make_async_copy(pe_buf.at[sl], pe_buf.at[sl], sems.at[1, sl]).wait()

    @pl.when(b == 0)
    def _():
        start_pages(0, 0)

    @pl.when(b + 1 < nb)
    def _():
        start_pages(b + 1, 1 - slot)

    wait_pages(slot)

    pad = jnp.zeros((_Q_ROWS - MLA_HEADS, LANES), F32)
    ql = jnp.concatenate([ql_ref[0], pad], axis=0)
    qp = jnp.concatenate([qp_ref[0], pad], axis=0)[:, :MLA_ROPE]
    lat_b = lat_buf[slot]
    pe_b = pe_buf[slot]
    s = (lax.dot_general(ql, lat_b, (((1,), (1,)), ((), ())), preferred_element_type=F32)
         + jnp.dot(qp, pe_b, preferred_element_type=F32))
    lat_new = latn_ref[0]
    s_new = (jnp.sum(ql * lat_new, axis=1, keepdims=True)
             + jnp.sum(qp * pen_ref[0], axis=1, keepdims=True))
    mx = jnp.maximum(jnp.max(s, axis=1, keepdims=True), s_new)
    p = jnp.exp2(s - mx)
    p_new = jnp.exp2(s_new - mx)
    denom = jnp.sum(p, axis=1, keepdims=True) + p_new
    o = (jnp.dot(p, lat_b, preferred_element_type=F32) + p_new * lat_new) / denom
    o_ref[0] = o[:MLA_HEADS]


def _mla_decode(mq, lat_new, pe_new, cache_kv, cache_pe, page_table, w_uk_t, w_uv_h):
    nb, n_pages = page_table.shape
    page = cache_kv.shape[1]
    past = n_pages * page
    hw = MLA_HEADS * LANES
    qlat, qpe = pl.pallas_call(
        _mla_qprep_kernel,
        out_shape=[jax.ShapeDtypeStruct((nb, hw), F32), jax.ShapeDtypeStruct((nb, hw), F32)],
        name="mla_qprep",
    )(mq, w_uk_t)
    per_b = lambda rows, width: pl.BlockSpec((1, rows, width), lambda b, pt: (b, 0, 0))
    grid_spec = pltpu.PrefetchScalarGridSpec(
        num_scalar_prefetch=1,
        grid=(nb,),
        in_specs=[per_b(MLA_HEADS, LANES), per_b(MLA_HEADS, LANES), per_b(1, MLA_KV_RANK), per_b(1, MLA_ROPE),
                  pl.BlockSpec(memory_space=pl.ANY), pl.BlockSpec(memory_space=pl.ANY)],
        out_specs=per_b(MLA_HEADS, MLA_KV_RANK),
        scratch_shapes=[pltpu.VMEM((2, past, MLA_KV_RANK), F32), pltpu.VMEM((2, MLA_ROPE, past), F32),
                        pltpu.SemaphoreType.DMA((2, 2))],
    )
    o_lat = pl.pallas_call(
        functools.partial(_mla_decode_kernel, page=page, n_pages=n_pages),
        grid_spec=grid_spec,
        out_shape=jax.ShapeDtypeStruct((nb, MLA_HEADS, MLA_KV_RANK), F32),
        compiler_params=_params("arbitrary"),
        name="mla_decode",
    )(page_table, qlat.reshape(nb, MLA_HEADS, LANES), qpe.reshape(nb, MLA_HEADS, LANES),
      lat_new.reshape(nb, 1, MLA_KV_RANK), pe_new.reshape(nb, 1, MLA_ROPE), cache_kv, cache_pe)
    return pl.pallas_call(
        _mla_oproj_kernel,
        out_shape=jax.ShapeDtypeStruct((nb, MLA_HEADS * MLA_V), BF16),
        name="mla_oproj",
    )(o_lat.reshape(nb, hw), w_uv_h)


def _rope_tables(pos):
    half = MLA_ROPE // 2
    inv_freq = jnp.power(jnp.float32(ROPE_THETA), -jnp.arange(half, dtype=F32) / half)
    ang = pos.astype(F32)[:, None] * inv_freq[None, :]
    cos, sin = jnp.cos(ang), jnp.sin(ang)
    z = jnp.zeros((pos.shape[0], LANES - MLA_ROPE), F32)
    return jnp.concatenate([cos, cos, z], axis=1), jnp.concatenate([sin, sin, z], axis=1)


def _rot_cols(w):
    half = MLA_ROPE // 2
    return jnp.concatenate([-w[:, half:], w[:, :half]], axis=1)


def _prep_mixer_weights(w_in, w_a_up, b_a, q_norm, w_uq, kv_norm, w_uk, w_uv):
    d = w_in.shape[0]
    gw, vw = GLA_HEADS * GLA_DK, GLA_HEADS * GLA_DV
    offs, o = [], 0
    for n in (gw, gw, vw, vw, GLA_GATE_RANK, MLA_Q_RANK, MLA_KV_RANK, MLA_ROPE):
        offs.append((o, o + n))
        o += n
    q, k, v, g, a_low, c_q, c_kv, k_pe = (w_in[:, a:b] for a, b in offs)
    zeros = lambda rows, n: jnp.zeros((rows, n), F32)
    z_half = zeros(d, LANES - MLA_ROPE)
    w_in_x = jnp.concatenate([q, k, v, g, c_q, c_kv, k_pe, z_half, _rot_cols(k_pe), z_half,
                              a_low, zeros(d, LANES - GLA_GATE_RANK)], axis=1).astype(BF16)
    w_a = jnp.concatenate([w_a_up, zeros(LANES - GLA_GATE_RANK, gw)], axis=0).astype(BF16)
    zq = zeros(MLA_Q_RANK, LANES - MLA_ROPE)
    cols = []
    for hd in range(MLA_HEADS):
        c0 = hd * (MLA_NOPE + MLA_ROPE)
        pe = w_uq[:, c0 + MLA_NOPE:c0 + MLA_NOPE + MLA_ROPE]
        cols += [w_uq[:, c0:c0 + MLA_NOPE], pe, zq, _rot_cols(pe), zq]
    w_uq_x = jnp.concatenate(cols, axis=1).astype(BF16)
    return {"w_in": w_in_x, "w_a": w_a, "b_a": b_a.reshape(1, gw), "q_norm": q_norm.reshape(1, -1),
            "w_uq": w_uq_x, "kv_norm": kv_norm.reshape(1, -1),
            "w_ukv": jnp.concatenate([w_uk.reshape(MLA_KV_RANK, -1), w_uv.reshape(MLA_KV_RANK, -1)],
                                     axis=1).astype(BF16)}


def kernel(x_prompt, x_sample, cache_kv, cache_pe, state_gla, page_table, ffn1_norm_w, ffn1_w_gate, ffn1_w_up, ffn1_w_down, mix_norm_w, w_in, gla_w_a_up, gla_b_a, gla_norm_w, mla_q_norm_w, mla_w_uq, mla_kv_norm_w, mla_w_uk, mla_w_uv, w_out, ffn2_norm_w, ffn2_w_gate, ffn2_w_up, ffn2_w_down, final_norm_w):
    assert cache_kv.shape[0] == 1, "single-layer trunk"
    batch, seq, d = x_prompt.shape
    nb, dec_seq, _ = x_sample.shape
    assert dec_seq == 1
    page = cache_kv.shape[2]
    past = page_table.shape[1] * page

    f1 = (ffn1_w_gate[0].astype(BF16), ffn1_w_up[0].astype(BF16), ffn1_w_down[0].astype(BF16))
    f2 = (ffn2_w_gate[0].astype(BF16), ffn2_w_up[0].astype(BF16), ffn2_w_down[0].astype(BF16))
    wo = w_out[0].astype(BF16)
    mw = _prep_mixer_weights(w_in[0], gla_w_a_up[0], gla_b_a[0], mla_q_norm_w[0], mla_w_uq[0],
                             mla_kv_norm_w[0], mla_w_uk[0], mla_w_uv[0])
    w_uk_t = jnp.transpose(mla_w_uk[0], (1, 2, 0)).astype(BF16)
    w_uv_h = jnp.transpose(mla_w_uv[0], (1, 0, 2)).astype(BF16)

    tm, tm_ffn = 512, 512
    xp = x_prompt.reshape(batch * seq, d)
    xp = _ffn(xp, ffn1_norm_w[0], *f1, tm=tm_ffn)
    cos, sin = _rope_tables(jnp.arange(seq, dtype=jnp.int32))
    gq, gk, gla, gv, gg, mq, mk, mv, lat, kpe = _inproj(xp, mix_norm_w[0], mw, cos, sin, tm=tm)
    y_gla, s_prompt = _gla_prompt(gq, gk, gla, gv, gg, gla_norm_w[0], batch=batch, tc=1024)
    y_mla = _mla_prompt(mq, mk, mv, batch=batch, tq=256)
    yp = _ffn(xp, ffn2_norm_w[0], *f2, mix=(y_gla, y_mla, wo), final_w=final_norm_w, tm=tm_ffn)

    xs = x_sample.reshape(nb, d)
    xs = _ffn(xs, ffn1_norm_w[0], *f1, tm=nb)
    cos_s, sin_s = _rope_tables(jnp.full((nb,), past, dtype=jnp.int32))
    sq, sk, sla, sv, sg, smq, _, _, slat, skpe = _inproj(xs, mix_norm_w[0], mw, cos_s, sin_s, tm=nb)
    ys_gla, s_sample = _gla_sample(sq, sk, sla, sv, sg, gla_norm_w[0], state_gla[0])
    ys_mla = _mla_decode(smq, slat, skpe, cache_kv[0], jnp.swapaxes(cache_pe[0], 1, 2), page_table,
                         w_uk_t, w_uv_h)
    ys = _ffn(xs, ffn2_norm_w[0], *f2, mix=(ys_gla, ys_mla, wo), final_w=final_norm_w, tm=nb)

    return (yp.reshape(batch, seq, d), ys.reshape(nb, 1, d),
            lat.reshape(1, batch, seq, MLA_KV_RANK), kpe.reshape(1, batch, seq, MLA_ROPE),
            s_prompt[None], slat.reshape(1, nb, 1, MLA_KV_RANK), skpe.reshape(1, nb, 1, MLA_ROPE),
            s_sample[None])
```

```python
import functools
import math

import jax
import jax.numpy as jnp
from jax import lax
from jax.experimental import pallas as pl
from jax.experimental.pallas import tpu as pltpu

F32 = jnp.float32
BF16 = jnp.bfloat16

EPS = 1e-6
GLA_HEADS = 4
GLA_DK = 64
GLA_DV = 128
GLA_GATE_RANK = 16
GLA_GATE_NORM = 16.0
MLA_HEADS = 4
MLA_NOPE = 128
MLA_ROPE = 64
MLA_V = 128
MLA_Q_RANK = 256
MLA_KV_RANK = 128
ROPE_THETA = 10000.0
MLA_SCALE = 1.0 / math.sqrt(MLA_NOPE + MLA_ROPE)
MLA_Q_SCALE = MLA_SCALE * math.log2(math.e)

LANES = 128
GLA_CHUNK = 128
GLA_PAIR = 2 * GLA_DK
GLA_MAX_CHUNK_DECAY = 60.0
VMEM_LIMIT = 56 * 1024 * 1024


def _rms(x, w):
    return x * lax.rsqrt(jnp.mean(x * x, axis=-1, keepdims=True) + EPS) * w


def _silu(x):
    return x * jax.nn.sigmoid(x)


def _const_spec(shape):
    nd = len(shape)
    return pl.BlockSpec(shape, lambda *_: (0,) * nd, pipeline_mode=pl.Buffered(1))


def _params(*sem):
    return pltpu.CompilerParams(dimension_semantics=sem, vmem_limit_bytes=VMEM_LIMIT)


def _ffn_kernel(*refs, has_mix, has_final):
    refs = list(refs)
    x_ref = refs.pop(0)
    if has_mix:
        ya_ref, yb_ref, wo_ref = refs.pop(0), refs.pop(0), refs.pop(0)
    nw_ref, wg_ref, wu_ref, wd_ref = refs.pop(0), refs.pop(0), refs.pop(0), refs.pop(0)
    if has_final:
        fw_ref = refs.pop(0)
    (o_ref,) = refs

    x = x_ref[...]
    if has_mix:
        y = jnp.concatenate([ya_ref[...], yb_ref[...]], axis=-1)
        x = x + jnp.dot(y, wo_ref[...], preferred_element_type=F32)
    h = _rms(x, nw_ref[...]).astype(BF16)
    g = jnp.dot(h, wg_ref[...], preferred_element_type=F32)
    u = jnp.dot(h, wu_ref[...], preferred_element_type=F32)
    a = (_silu(g) * u).astype(BF16)
    x = x + 0.5 * jnp.dot(a, wd_ref[...], preferred_element_type=F32)
    if has_final:
        x = _rms(x, fw_ref[...])
    o_ref[...] = x


def _ffn(x, norm_w, wg, wu, wd, *, mix=None, final_w=None, tm):
    m, d = x.shape
    ff = wg.shape[1]
    tm = min(tm, m)
    row = lambda i: (i, 0)
    args, specs = [x], [pl.BlockSpec((tm, d), row)]
    if mix is not None:
        ya, yb, wo = mix
        args += [ya, yb, wo]
        specs += [pl.BlockSpec((tm, ya.shape[1]), row), pl.BlockSpec((tm, yb.shape[1]), row),
                  _const_spec(wo.shape)]
    args += [norm_w.reshape(1, d), wg, wu, wd]
    specs += [_const_spec((1, d)), _const_spec((d, ff)), _const_spec((d, ff)), _const_spec((ff, d))]
    if final_w is not None:
        args.append(final_w.reshape(1, d))
        specs.append(_const_spec((1, d)))
    return pl.pallas_call(
        functools.partial(_ffn_kernel, has_mix=mix is not None, has_final=final_w is not None),
        grid=(m // tm,),
        in_specs=specs,
        out_specs=pl.BlockSpec((tm, d), row),
        out_shape=jax.ShapeDtypeStruct((m, d), F32),
        compiler_params=_params("parallel"),
        name="ffn_mix_final" if mix is not None else "ffn",
    )(*args)


_IN_GQ, _IN_GK, _IN_GV, _IN_GG, _IN_CQ, _IN_CKV, _IN_KPE, _IN_KROT, _IN_ALOW, _IN_END = (
    0, 256, 512, 1024, 1536, 1792, 1920, 2048, 2176, 2304)
_UQ_HEAD = 3 * LANES
_QK_HEAD = 2 * LANES
_Q_ROWS = 16


def _inproj_kernel(x_ref, nw_ref, win_ref, wa_ref, ba_ref, qn_ref, wuq_ref, kvn_ref, wukv_ref,
                   cos_ref, sin_ref,
                   gq_ref, gk_ref, gla_ref, gv_ref, gg_ref, mq_ref, mk_ref, mv_ref, lat_ref, kpe_ref):
    h = _rms(x_ref[...], nw_ref[...]).astype(BF16)
    p = jnp.dot(h, win_ref[...], preferred_element_type=F32)
    cos, sin = cos_ref[...], sin_ref[...]

    gq_ref[...] = p[:, _IN_GQ:_IN_GK] * (GLA_DK ** -0.5)
    gk_ref[...] = p[:, _IN_GK:_IN_GV]
    gv_ref[...] = p[:, _IN_GV:_IN_GG].astype(BF16)
    gg_ref[...] = p[:, _IN_GG:_IN_CQ]
    z = jnp.dot(p[:, _IN_ALOW:_IN_END].astype(BF16), wa_ref[...], preferred_element_type=F32) + ba_ref[...]
    log_sig = jnp.minimum(z, 0.0) - jnp.log1p(jnp.exp(-jnp.abs(z)))
    gla_ref[...] = log_sig * (1.0 / GLA_GATE_NORM)

    cq = _rms(p[:, _IN_CQ:_IN_CKV], qn_ref[...]).astype(BF16)
    qx = jnp.dot(cq, wuq_ref[...], preferred_element_type=F32)
    lat = _rms(p[:, _IN_CKV:_IN_KPE], kvn_ref[...])
    lat_ref[...] = lat
    kpe = p[:, _IN_KPE:_IN_KROT] * cos + p[:, _IN_KROT:_IN_ALOW] * sin
    kpe_ref[...] = kpe[:, :MLA_ROPE]
    kv = jnp.dot(lat.astype(BF16), wukv_ref[...], preferred_element_type=F32)
    kpe_b = kpe.astype(BF16)
    for hd in range(MLA_HEADS):
        q0 = hd * _UQ_HEAD
        q_pe = qx[:, q0 + LANES:q0 + 2 * LANES] * cos + qx[:, q0 + 2 * LANES:q0 + 3 * LANES] * sin
        o0 = hd * _QK_HEAD
        mq_ref[:, o0:o0 + LANES] = (qx[:, q0:q0 + LANES] * MLA_Q_SCALE).astype(BF16)
        mq_ref[:, o0 + LANES:o0 + 2 * LANES] = (q_pe * MLA_Q_SCALE).astype(BF16)
        mk_ref[:, o0:o0 + LANES] = kv[:, hd * MLA_NOPE:(hd + 1) * MLA_NOPE].astype(BF16)
        mk_ref[:, o0 + LANES:o0 + 2 * LANES] = kpe_b
    mv_ref[...] = kv[:, MLA_HEADS * MLA_NOPE:].astype(BF16)


def _inproj(x, norm_w, w, cos, sin, *, tm):
    m, d = x.shape
    tm = min(tm, m)
    n_tab = cos.shape[0] // tm
    row = lambda i: (i, 0)
    tab = lambda i: (i % n_tab, 0)
    gw, mw = GLA_HEADS * GLA_DK, MLA_HEADS * _QK_HEAD
    outs = [((m, gw), F32), ((m, gw), F32), ((m, gw), F32), ((m, GLA_HEADS * GLA_DV), BF16),
            ((m, GLA_HEADS * GLA_DV), F32), ((m, mw), BF16), ((m, mw), BF16),
            ((m, MLA_HEADS * MLA_V), BF16), ((m, MLA_KV_RANK), F32), ((m, MLA_ROPE), F32)]
    consts = [norm_w.reshape(1, d), w["w_in"], w["w_a"], w["b_a"], w["q_norm"], w["w_uq"],
              w["kv_norm"], w["w_ukv"]]
    out_specs = [pl.BlockSpec((tm, s[1]), row) for s, _ in outs]
    return pl.pallas_call(
        _inproj_kernel,
        grid=(m // tm,),
        in_specs=[pl.BlockSpec((tm, d), row)] + [_const_spec(c.shape) for c in consts]
        + [pl.BlockSpec((tm, LANES), tab), pl.BlockSpec((tm, LANES), tab)],
        out_specs=out_specs,
        out_shape=[jax.ShapeDtypeStruct(s, dt) for s, dt in outs],
        compiler_params=_params("parallel"),
        name="inproj",
    )(x, *consts, cos, sin)


def _split3(x):
    hi = x.astype(BF16)
    r = x - hi.astype(F32)
    mid = r.astype(BF16)
    lo = (r - mid.astype(F32)).astype(BF16)
    return hi, mid, lo


def _gla_prompt_kernel(q_ref, k_ref, la_ref, v_ref, g_ref, gn_ref, y_ref, s_ref, o_scr):
    c_sz = GLA_CHUNK
    n_chunks = q_ref.shape[1] // c_sz

    @pl.when(pl.program_id(1) == 0)
    def _():
        s_ref[...] = jnp.zeros_like(s_ref)

    row = lax.broadcasted_iota(jnp.int32, (c_sz, c_sz), 0)
    col = lax.broadcasted_iota(jnp.int32, (c_sz, c_sz), 1)
    tri = row >= col
    tri_b = jnp.where(tri, 1.0, 0.0).astype(BF16)
    lane_lo = col < GLA_DK
    row_lo = row < GLA_DK
    gn = gn_ref[...]

    def finish(rows, hd, o, g):
        hs = slice(hd * GLA_DV, (hd + 1) * GLA_DV)
        y_ref[0, rows, hs] = (_rms(o, gn) * _silu(g[:, hs])).astype(BF16)

    def exact_chunk(c, carry):
        rows = pl.ds(pl.multiple_of(c * c_sz, c_sz), c_sz)
        a_t = jnp.exp(la_ref[0, rows, :]).T
        k_t = k_ref[0, rows, :].T
        q_t = q_ref[0, rows, :].T
        v = v_ref[0, rows, :].astype(F32)
        tok_lane = lax.broadcasted_iota(jnp.int32, a_t.shape, 1)
        tok_row = lax.broadcasted_iota(jnp.int32, v.shape, 0)
        o_scr[...] = jnp.zeros_like(o_scr)

        def token(t, carry_t):
            pick = tok_lane == t
            a_c = jnp.sum(jnp.where(pick, a_t, 0.0), axis=1, keepdims=True)
            k_c = jnp.sum(jnp.where(pick, k_t, 0.0), axis=1, keepdims=True)
            q_c = jnp.sum(jnp.where(pick, q_t, 0.0), axis=1, keepdims=True)
            v_r = jnp.sum(jnp.where(tok_row == t, v, 0.0), axis=0, keepdims=True)
            o_heads = []
            for pr in range(GLA_HEADS // 2):
                sl = slice(pr * GLA_PAIR, (pr + 1) * GLA_PAIR)
                v_lo = v_r[:, (2 * pr) * GLA_DV:(2 * pr + 1) * GLA_DV]
                v_hi = v_r[:, (2 * pr + 1) * GLA_DV:(2 * pr + 2) * GLA_DV]
                s_new = a_c[sl] * s_ref[0, pr] + k_c[sl] * jnp.where(row_lo, v_lo, v_hi)
                s_ref[0, pr] = s_new
                qs = q_c[sl] * s_new
                o_heads.append(jnp.sum(jnp.where(row_lo, qs, 0.0), axis=0, keepdims=True))
                o_heads.append(jnp.sum(jnp.where(row_lo, 0.0, qs), axis=0, keepdims=True))
            o_scr[...] = jnp.where(tok_row == t, jnp.concatenate(o_heads, axis=1), o_scr[...])
            return carry_t

        lax.fori_loop(0, c_sz, token, 0)
        g = g_ref[0, rows, :]
        for hd in range(GLA_HEADS):
            finish(rows, hd, o_scr[:, hd * GLA_DV:(hd + 1) * GLA_DV], g)
        return carry

    def chunk(c, carry):
        r0 = pl.multiple_of(c * c_sz, c_sz)
        rows = pl.ds(r0, c_sz)
        la = la_ref[0, rows, :]
        b = sum(jnp.dot(tri_b, t, preferred_element_type=F32) for t in _split3(la))
        e_pos = jnp.exp(b)
        qp = q_ref[0, rows, :] * e_pos
        kp = k_ref[0, rows, :] * jnp.exp(-b)
        kd = kp * e_pos[c_sz - 1:c_sz, :]
        e_last_col = jnp.exp(jnp.sum(la.T, axis=1, keepdims=True))
        v = v_ref[0, rows, :]
        g = g_ref[0, rows, :]
        for pr in range(GLA_HEADS // 2):
            sl = slice(pr * GLA_PAIR, (pr + 1) * GLA_PAIR)
            qp_p = qp[:, sl]
            q_heads = (jnp.where(lane_lo, qp_p, 0.0).astype(BF16), jnp.where(lane_lo, 0.0, qp_p).astype(BF16))
            attn = lax.dot_general(jnp.concatenate(q_heads, axis=0), kp[:, sl].astype(BF16),
                                   (((1,), (1,)), ((), ())), preferred_element_type=F32)
            s_old = s_ref[0, pr]
            s_old_b = s_old.astype(BF16)
            for j in range(2):
                hd = 2 * pr + j
                hs = slice(hd * GLA_DV, (hd + 1) * GLA_DV)
                a_j = jnp.where(tri, attn[j * c_sz:(j + 1) * c_sz], 0.0).astype(BF16)
                o = jnp.dot(jnp.concatenate([a_j, q_heads[j]], axis=1),
                            jnp.concatenate([v[:, hs], s_old_b], axis=0), preferred_element_type=F32)
                finish(rows, hd, o, g)
            kd_t = kd[:, sl].T
            kd_heads = jnp.concatenate([jnp.where(row_lo, kd_t, 0.0).astype(BF16),
                                        jnp.where(row_lo, 0.0, kd_t).astype(BF16)], axis=1)
            v_pair = jnp.concatenate([v[:, (2 * pr) * GLA_DV:(2 * pr + 1) * GLA_DV],
                                      v[:, (2 * pr + 1) * GLA_DV:(2 * pr + 2) * GLA_DV]], axis=0)
            s_ref[0, pr] = e_last_col[sl] * s_old + jnp.dot(kd_heads, v_pair, preferred_element_type=F32)
        return carry

    la_all = la_ref[0]
    total = jnp.concatenate([jnp.sum(la_all[c * c_sz:(c + 1) * c_sz], axis=0, keepdims=True)
                             for c in range(n_chunks)], axis=0)
    chunked_ok = jnp.min(total) >= -GLA_MAX_CHUNK_DECAY

    @pl.when(chunked_ok)
    def _():
        lax.fori_loop(0, n_chunks, chunk, 0, unroll=True)

    @pl.when(jnp.logical_not(chunked_ok))
    def _():
        lax.fori_loop(0, n_chunks, exact_chunk, 0)


def _gla_prompt(gq, gk, gla, gv, gg, g_norm, *, batch, tc):
    m = gq.shape[0]
    t = m // batch
    tc = min(tc, t)
    r3 = lambda a: a.reshape(batch, t, a.shape[1])
    blk = lambda width: pl.BlockSpec((1, tc, width), lambda b, i: (b, i, 0))
    kw, vw = GLA_HEADS * GLA_DK, GLA_HEADS * GLA_DV
    y, s = pl.pallas_call(
        _gla_prompt_kernel,
        grid=(batch, t // tc),
        in_specs=[blk(kw), blk(kw), blk(kw), blk(vw), blk(vw), _const_spec((1, GLA_DV))],
        out_specs=[blk(vw), pl.BlockSpec((1, GLA_HEADS // 2, GLA_PAIR, GLA_DV), lambda b, i: (b, 0, 0, 0))],
        out_shape=[jax.ShapeDtypeStruct((batch, t, vw), BF16),
                   jax.ShapeDtypeStruct((batch, GLA_HEADS // 2, GLA_PAIR, GLA_DV), F32)],
        scratch_shapes=[pltpu.VMEM((GLA_CHUNK, vw), F32)],
        compiler_params=_params("parallel", "arbitrary"),
        name="gla_prompt",
    )(r3(gq), r3(gk), r3(gla), r3(gv), r3(gg), g_norm.reshape(1, GLA_DV))
    return y.reshape(m, vw), s.reshape(batch, GLA_HEADS, GLA_DK, GLA_DV)


def _gla_sample_kernel(q_ref, k_ref, la_ref, v_ref, g_ref, gn_ref, s_ref, y_ref, so_ref):
    nb = q_ref.shape[0]
    q, k = q_ref[...], k_ref[...]
    a = jnp.exp(la_ref[...])
    qa = q * a
    v = v_ref[...].astype(F32)
    g = g_ref[...]
    lane_lo = lax.broadcasted_iota(jnp.int32, (nb, GLA_PAIR), 1) < GLA_DK
    qk = q * k
    qk_heads = (jnp.sum(jnp.where(lane_lo, qk, 0.0), axis=1, keepdims=True),
                jnp.sum(jnp.where(lane_lo, 0.0, qk), axis=1, keepdims=True))
    o = [qk_heads[j] * v[:, j * GLA_DV:(j + 1) * GLA_DV] for j in range(2)]
    for c in range(GLA_PAIR):
        j = c // GLA_DK
        s_c = s_ref[:, c, :]
        o[j] = o[j] + qa[:, c:c + 1] * s_c
        so_ref[:, c, :] = a[:, c:c + 1] * s_c + k[:, c:c + 1] * v[:, j * GLA_DV:(j + 1) * GLA_DV]
    gn = gn_ref[...]
    for j in range(2):
        hs = slice(j * GLA_DV, (j + 1) * GLA_DV)
        y_ref[:, hs] = (_rms(o[j], gn) * _silu(g[:, hs])).astype(BF16)


def _gla_sample(gq, gk, gla, gv, gg, g_norm, state):
    nb = gq.shape[0]
    kw, vw = GLA_HEADS * GLA_DK, GLA_HEADS * GLA_DV
    st = state.reshape(nb, kw, GLA_DV)
    lane_blk = lambda width: pl.BlockSpec((nb, width), lambda p: (0, p))
    st_blk = pl.BlockSpec((nb, GLA_PAIR, GLA_DV), lambda p: (0, p, 0))
    y, s = pl.pallas_call(
        _gla_sample_kernel,
        grid=(GLA_HEADS // 2,),
        in_specs=[lane_blk(GLA_PAIR), lane_blk(GLA_PAIR), lane_blk(GLA_PAIR), lane_blk(2 * GLA_DV),
                  lane_blk(2 * GLA_DV), _const_spec((1, GLA_DV)), st_blk],
        out_specs=[lane_blk(2 * GLA_DV), st_blk],
        out_shape=[jax.ShapeDtypeStruct((nb, vw), BF16), jax.ShapeDtypeStruct((nb, kw, GLA_DV), F32)],
        compiler_params=_params("parallel"),
        name="gla_sample",
    )(gq, gk, gla, gv, gg, g_norm.reshape(1, GLA_DV), st)
    return y, s.reshape(nb, GLA_HEADS, GLA_DK, GLA_DV)


def _mla_prompt_kernel(q_ref, k_ref, v_ref, o_ref, *, tq):
    t = q_ref.shape[1]
    contract_last = (((1,), (1,)), ((), ()))
    causal = (lax.broadcasted_iota(jnp.int32, (tq, tq), 1) <= lax.broadcasted_iota(jnp.int32, (tq, tq), 0))
    for qi in range(t // tq):
        q0, kend = qi * tq, (qi + 1) * tq
        q = q_ref[0, q0:kend, :]
        s_d = lax.dot_general(q, k_ref[0, q0:kend, :], contract_last, preferred_element_type=F32)
        s_d = jnp.where(causal, s_d, -jnp.inf)
        mx = jnp.max(s_d, axis=1, keepdims=True)
        if qi:
            s_o = lax.dot_general(q, k_ref[0, 0:q0, :], contract_last, preferred_element_type=F32)
            mx = jnp.maximum(mx, jnp.max(s_o, axis=1, keepdims=True))
        p_d = jnp.exp2(s_d - mx)
        denom = jnp.sum(p_d, axis=1, keepdims=True)
        o = jnp.dot(p_d.astype(BF16), v_ref[0, q0:kend, :], preferred_element_type=F32)
        if qi:
            p_o = jnp.exp2(s_o - mx)
            denom = denom + jnp.sum(p_o, axis=1, keepdims=True)
            o = o + jnp.dot(p_o.astype(BF16), v_ref[0, 0:q0, :], preferred_element_type=F32)
        o_ref[0, q0:kend, :] = (o / denom).astype(BF16)


def _mla_prompt(mq, mk, mv, *, batch, tq):
    m = mq.shape[0]
    t = m // batch
    tq = min(tq, t)
    qk_blk = pl.BlockSpec((1, t, _QK_HEAD), lambda b, h: (b, 0, h))
    v_blk = pl.BlockSpec((1, t, MLA_V), lambda b, h: (b, 0, h))
    y = pl.pallas_call(
        functools.partial(_mla_prompt_kernel, tq=tq),
        grid=(batch, MLA_HEADS),
        in_specs=[qk_blk, qk_blk, v_blk],
        out_specs=v_blk,
        out_shape=jax.ShapeDtypeStruct((batch, t, MLA_HEADS * MLA_V), BF16),
        compiler_params=_params("parallel", "parallel"),
        name="mla_prompt",
    )(mq.reshape(batch, t, -1), mk.reshape(batch, t, -1), mv.reshape(batch, t, -1))
    return y.reshape(m, MLA_HEADS * MLA_V)


def _mla_qprep_kernel(mq_ref, wuk_ref, qlat_ref, qpe_ref):
    for hd in range(MLA_HEADS):
        q0 = hd * _QK_HEAD
        hs = slice(hd * LANES, (hd + 1) * LANES)
        qlat_ref[:, hs] = jnp.dot(mq_ref[:, q0:q0 + LANES], wuk_ref[hd], preferred_element_type=F32)
        qpe_ref[:, hs] = mq_ref[:, q0 + LANES:q0 + 2 * LANES].astype(F32)


def _mla_oproj_kernel(ol_ref, wuv_ref, y_ref):
    for hd in range(MLA_HEADS):
        hs = slice(hd * LANES, (hd + 1) * LANES)
        y_ref[:, hs] = jnp.dot(ol_ref[:, hs].astype(BF16), wuv_ref[hd],
                               preferred_element_type=F32).astype(BF16)


def _mla_decode_kernel(pt_ref, ql_ref, qp_ref, latn_ref, pen_ref, ckv_ref, cpe_ref,
                       o_ref, lat_buf, pe_buf, sems, *, page, n_pages):
    b = pl.program_id(0)
    nb = pl.num_programs(0)
    slot = b % 2

    def page_copies(bb, sl, p):
        pg = pt_ref[bb, p]
        rows = pl.ds(pl.multiple_of(p * page, page), page)
        return (pltpu.make_async_copy(ckv_ref.at[pg], lat_buf.at[sl, rows, :], sems.at[0, sl]),
                pltpu.make_async_copy(cpe_ref.at[pg], pe_buf.at[sl, :, rows], sems.at[1, sl]))

    def start_pages(bb, sl):
        def body(p, carry):
            for cp in page_copies(bb, sl, p):
                cp.start()
            return carry
        lax.fori_loop(0, n_pages, body, 0, unroll=8)

    def wait_pages(sl):
        pltpu.make_async_copy(lat_buf.at[sl], lat_buf.at[sl], sems.at[0, sl]).wait()
        pltpu.make_async_copy(pe_buf.at[sl], pe_buf.at[sl], sems.at[1, sl]).wait()

    @pl.when(b == 0)
    def _():
        start_pages(0, 0)

    @pl.when(b + 1 < nb)
    def _():
        start_pages(b + 1, 1 - slot)

    wait_pages(slot)

    pad = jnp.zeros((_Q_ROWS - MLA_HEADS, LANES), F32)
    ql = jnp.concatenate([ql_ref[0], pad], axis=0)
    qp = jnp.concatenate([qp_ref[0], pad], axis=0)[:, :MLA_ROPE]
    lat_b = lat_buf[slot]
    pe_b = pe_buf[slot]
    s = (lax.dot_general(ql, lat_b, (((1,), (1,)), ((), ())), preferred_element_type=F32)
         + jnp.dot(qp, pe_b, preferred_element_type=F32))
    lat_new = latn_ref[0]
    s_new = (jnp.sum(ql * lat_new, axis=1, keepdims=True)
             + jnp.sum(qp * pen_ref[0], axis=1, keepdims=True))
    mx = jnp.maximum(jnp.max(s, axis=1, keepdims=True), s_new)
    p = jnp.exp2(s - mx)
    p_new = jnp.exp2(s_new - mx)
    denom = jnp.sum(p, axis=1, keepdims=True) + p_new
    o = (jnp.dot(p, lat_b, preferred_element_type=F32) + p_new * lat_new) / denom
    o_ref[0] = o[:MLA_HEADS]


def _mla_decode(mq, lat_new, pe_new, cache_kv, cache_pe, page_table, w_uk_t, w_uv_h):
    nb, n_pages = page_table.shape
    page = cache_kv.shape[1]
    past = n_pages * page
    hw = MLA_HEADS * LANES
    qlat, qpe = pl.pallas_call(
        _mla_qprep_kernel,
        out_shape=[jax.ShapeDtypeStruct((nb, hw), F32), jax.ShapeDtypeStruct((nb, hw), F32)],
        name="mla_qprep",
    )(mq, w_uk_t)
    per_b = lambda rows, width: pl.BlockSpec((1, rows, width), lambda b, pt: (b, 0, 0))
    grid_spec = pltpu.PrefetchScalarGridSpec(
        num_scalar_prefetch=1,
        grid=(nb,),
        in_specs=[per_b(MLA_HEADS, LANES), per_b(MLA_HEADS, LANES), per_b(1, MLA_KV_RANK), per_b(1, MLA_ROPE),
                  pl.BlockSpec(memory_space=pl.ANY), pl.BlockSpec(memory_space=pl.ANY)],
        out_specs=per_b(MLA_HEADS, MLA_KV_RANK),
        scratch_shapes=[pltpu.VMEM((2, past, MLA_KV_RANK), F32), pltpu.VMEM((2, MLA_ROPE, past), F32),
                        pltpu.SemaphoreType.DMA((2, 2))],
    )
    o_lat = pl.pallas_call(
        functools.partial(_mla_decode_kernel, page=page, n_pages=n_pages),
        grid_spec=grid_spec,
        out_shape=jax.ShapeDtypeStruct((nb, MLA_HEADS, MLA_KV_RANK), F32),
        compiler_params=_params("arbitrary"),
        name="mla_decode",
    )(page_table, qlat.reshape(nb, MLA_HEADS, LANES), qpe.reshape(nb, MLA_HEADS, LANES),
      lat_new.reshape(nb, 1, MLA_KV_RANK), pe_new.reshape(nb, 1, MLA_ROPE), cache_kv, cache_pe)
    return pl.pallas_call(
        _mla_oproj_kernel,
        out_shape=jax.ShapeDtypeStruct((nb, MLA_HEADS * MLA_V), BF16),
        name="mla_oproj",
    )(o_lat.reshape(nb, hw), w_uv_h)


def _rope_tables(pos):
    half = MLA_ROPE // 2
    inv_freq = jnp.power(jnp.float32(ROPE_THETA), -jnp.arange(half, dtype=F32) / half)
    ang = pos.astype(F32)[:, None] * inv_freq[None, :]
    cos, sin = jnp.cos(ang), jnp.sin(ang)
    z = jnp.zeros((pos.shape[0], LANES - MLA_ROPE), F32)
    return jnp.concatenate([cos, cos, z], axis=1), jnp.concatenate([sin, sin, z], axis=1)


def _rot_cols(w):
    half = MLA_ROPE // 2
    return jnp.concatenate([-w[:, half:], w[:, :half]], axis=1)


def _prep_mixer_weights(w_in, w_a_up, b_a, q_norm, w_uq, kv_norm, w_uk, w_uv):
    d = w_in.shape[0]
    gw, vw = GLA_HEADS * GLA_DK, GLA_HEADS * GLA_DV
    offs, o = [], 0
    for n in (gw, gw, vw, vw, GLA_GATE_RANK, MLA_Q_RANK, MLA_KV_RANK, MLA_ROPE):
        offs.append((o, o + n))
        o += n
    q, k, v, g, a_low, c_q, c_kv, k_pe = (w_in[:, a:b] for a, b in offs)
    zeros = lambda rows, n: jnp.zeros((rows, n), F32)
    z_half = zeros(d, LANES - MLA_ROPE)
    w_in_x = jnp.concatenate([q, k, v, g, c_q, c_kv, k_pe, z_half, _rot_cols(k_pe), z_half,
                              a_low, zeros(d, LANES - GLA_GATE_RANK)], axis=1).astype(BF16)
    w_a = jnp.concatenate([w_a_up, zeros(LANES - GLA_GATE_RANK, gw)], axis=0).astype(BF16)
    zq = zeros(MLA_Q_RANK, LANES - MLA_ROPE)
    cols = []
    for hd in range(MLA_HEADS):
        c0 = hd * (MLA_NOPE + MLA_ROPE)
        pe = w_uq[:, c0 + MLA_NOPE:c0 + MLA_NOPE + MLA_ROPE]
        cols += [w_uq[:, c0:c0 + MLA_NOPE], pe, zq, _rot_cols(pe), zq]
    w_uq_x = jnp.concatenate(cols, axis=1).astype(BF16)
    return {"w_in": w_in_x, "w_a": w_a, "b_a": b_a.reshape(1, gw), "q_norm": q_norm.reshape(1, -1),
            "w_uq": w_uq_x, "kv_norm": kv_norm.reshape(1, -1),
            "w_ukv": jnp.concatenate([w_uk.reshape(MLA_KV_RANK, -1), w_uv.reshape(MLA_KV_RANK, -1)],
                                     axis=1).astype(BF16)}


def kernel(x_prompt, x_sample, cache_kv, cache_pe, state_gla, page_table, ffn1_norm_w, ffn1_w_gate, ffn1_w_up, ffn1_w_down, mix_norm_w, w_in, gla_w_a_up, gla_b_a, gla_norm_w, mla_q_norm_w, mla_w_uq, mla_kv_norm_w, mla_w_uk, mla_w_uv, w_out, ffn2_norm_w, ffn2_w_gate, ffn2_w_up, ffn2_w_down, final_norm_w):
    assert cache_kv.shape[0] == 1, "single-layer trunk"
    batch, seq, d = x_prompt.shape
    nb, dec_seq, _ = x_sample.shape
    assert dec_seq == 1
    page = cache_kv.shape[2]
    past = page_table.shape[1] * page

    f1 = (ffn1_w_gate[0].astype(BF16), ffn1_w_up[0].astype(BF16), ffn1_w_down[0].astype(BF16))
    f2 = (ffn2_w_gate[0].astype(BF16), ffn2_w_up[0].astype(BF16), ffn2_w_down[0].astype(BF16))
    wo = w_out[0].astype(BF16)
    mw = _prep_mixer_weights(w_in[0], gla_w_a_up[0], gla_b_a[0], mla_q_norm_w[0], mla_w_uq[0],
                             mla_kv_norm_w[0], mla_w_uk[0], mla_w_uv[0])
    w_uk_t = jnp.transpose(mla_w_uk[0], (1, 2, 0)).astype(BF16)
    w_uv_h = jnp.transpose(mla_w_uv[0], (1, 0, 2)).astype(BF16)

    tm, tm_ffn = 1024, 512
    xp = x_prompt.reshape(batch * seq, d)
    xp = _ffn(xp, ffn1_norm_w[0], *f1, tm=tm_ffn)
    cos, sin = _rope_tables(jnp.arange(seq, dtype=jnp.int32))
    gq, gk, gla, gv, gg, mq, mk, mv, lat, kpe = _inproj(xp, mix_norm_w[0], mw, cos, sin, tm=tm)
    y_gla, s_prompt = _gla_prompt(gq, gk, gla, gv, gg, gla_norm_w[0], batch=batch, tc=1024)
    y_mla = _mla_prompt(mq, mk, mv, batch=batch, tq=256)
    yp = _ffn(xp, ffn2_norm_w[0], *f2, mix=(y_gla, y_mla, wo), final_w=final_norm_w, tm=tm_ffn)

    xs = x_sample.reshape(nb, d)
    xs = _ffn(xs, ffn1_norm_w[0], *f1, tm=nb)
    cos_s, sin_s = _rope_tables(jnp.full((nb,), past, dtype=jnp.int32))
    sq, sk, sla, sv, sg, smq, _, _, slat, skpe = _inproj(xs, mix_norm_w[0], mw, cos_s, sin_s, tm=nb)
    ys_gla, s_sample = _gla_sample(sq, sk, sla, sv, sg, gla_norm_w[0], state_gla[0])
    ys_mla = _mla_decode(smq, slat, skpe, cache_kv[0], jnp.swapaxes(cache_pe[0], 1, 2), page_table,
                         w_uk_t, w_uv_h)
    ys = _ffn(xs, ffn2_norm_w[0], *f2, mix=(ys_gla, ys_mla, wo), final_w=final_norm_w, tm=nb)

    return (yp.reshape(batch, seq, d), ys.reshape(nb, 1, d),
            lat.reshape(1, batch, seq, MLA_KV_RANK), kpe.reshape(1, batch, seq, MLA_ROPE),
            s_prompt[None], slat.reshape(1, nb, 1, MLA_KV_RANK), skpe.reshape(1, nb, 1, MLA_ROPE),
            s_sample[None])
```
